```python
import math
import jax, jax.numpy as jnp
from jax import lax
import numpy as np

D_MODEL = 4096
BATCH = 4
SEQ = 4096
DEPTH = 4
DEC_BATCH = 16
DEC_SEQ = 32
PAST_LEN = 2048

CHUNK = 64
N_EVEN = (DEPTH + 1) // 2
N_ODD = DEPTH // 2
MIX_WIDTH = D_MODEL
SB_WIDTH = MIX_WIDTH // 2
SB_HEAD_DIM = 128
SB_HEADS = SB_WIDTH // SB_HEAD_DIM
SB_QBLOCK = 128
SSM_WIDTH = MIX_WIDTH - SB_WIDTH
SSM_GROUP = 16
SSM_GROUPS = SSM_WIDTH // SSM_GROUP
SSM_STATE = 64
LRU_WIDTH = MIX_WIDTH // 2
LRU_BLOCK = 128
LRU_BLOCKS = LRU_WIDTH // LRU_BLOCK
LRU_CONV = 4
LRU_C = 8.0
CONF_WIDTH = MIX_WIDTH - LRU_WIDTH
CONF_CONV = 31
PEER_HEADS = 8
PEER_QDIM = 256
PEER_HALF = PEER_QDIM // 2
N_KEYS = 128
N_EXPERTS = N_KEYS * N_KEYS
PEER_TOPK = 16
PEER_BLOCK = 128
ALPHA = (2.0 * DEPTH) ** 0.25
BETA = (8.0 * DEPTH) ** -0.25
LN_EPS = 1e-5

kernel_name = 'streaming_sb_s5_rglru_conformer_peer_step'


def layer_norm(x, g, b):
    xf = x.astype(jnp.float32)
    mu = jnp.mean(xf, -1, keepdims=True)
    var = jnp.mean(jnp.square(xf - mu), -1, keepdims=True)
    return ((xf - mu) * lax.rsqrt(var + LN_EPS) * g + b).astype(x.dtype)


def modulation(c, w, b):
    m = (jax.nn.silu(c) @ w + b)[:, None, :]
    shift, scale, gate = jnp.split(m, 3, axis=-1)
    return shift, scale, gate


def causal_dwconv(x, hist, w, b):
    xp = jnp.concatenate([hist.astype(x.dtype), x], axis=1)
    y = lax.conv_general_dilated(xp, w[:, None, :].astype(x.dtype), window_strides=(1,), padding='VALID',
                                 dimension_numbers=('NWC', 'WIO', 'NWC'), feature_group_count=x.shape[-1])
    return y + b.astype(x.dtype), xp[:, -(w.shape[0] - 1):]


def stick_breaking_attention(q, k, v, k_past, v_past):
    P = k_past.shape[1]
    L = q.shape[1]
    k_all = jnp.concatenate([k_past.astype(k.dtype), k], axis=1)
    v_all = jnp.concatenate([v_past.astype(v.dtype), v], axis=1)
    qb_len = min(SB_QBLOCK, L)
    scale = SB_HEAD_DIM ** -0.5
    outs = []
    for start in range(0, L, qb_len):
        stop = min(start + qb_len, L)
        n_k = P + stop
        qb = q[:, start:stop].astype(jnp.float32)
        kb = k_all[:, :n_k].astype(jnp.float32)
        vb = v_all[:, :n_k].astype(jnp.float32)
        z = jnp.einsum('bqhd,bkhd->bhqk', qb, kb) * scale
        qpos = P + start + jnp.arange(stop - start)
        kpos = jnp.arange(n_k)
        mask = kpos[None, :] < qpos[:, None]
        log_keep = jnp.where(mask, jax.nn.log_sigmoid(-z), 0.0)
        later = lax.cumsum(log_keep, axis=3, reverse=True) - log_keep
        w = jnp.where(mask, jnp.exp(jax.nn.log_sigmoid(z) + later), 0.0)
        outs.append(jnp.einsum('bhqk,bkhd->bqhd', w, vb))
    return jnp.concatenate(outs, axis=1).astype(q.dtype)


def complex_combine(e1, e2):
    a1r, a1i, b1r, b1i = e1
    a2r, a2i, b2r, b2i = e2
    return (a2r * a1r - a2i * a1i,
            a2r * a1i + a2i * a1r,
            a2r * b1r - a2i * b1i + b2r,
            a2r * b1i + a2i * b1r + b2i)


def real_combine(e1, e2):
    a1, b1 = e1
    a2, b2 = e2
    return (a1 * a2, a2 * b1 + b2)


def s5_ssm(u, h0_re, h0_im, a_re, a_im, log_dt, b_re, b_im, c_re, c_im, d):
    f32 = jnp.float32
    Bsz, L, _ = u.shape
    dt = jnp.exp(log_dt.astype(f32))[:, None]
    ar, ai = a_re.astype(f32), a_im.astype(f32)
    mag = jnp.exp(ar * dt)
    abar_re, abar_im = mag * jnp.cos(ai * dt), mag * jnp.sin(ai * dt)
    den = ar * ar + ai * ai
    nr, ni = abar_re - 1.0, abar_im
    z_re, z_im = (nr * ar + ni * ai) / den, (ni * ar - nr * ai) / den
    br, bi = b_re.astype(f32), b_im.astype(f32)
    bb_re = z_re[..., None] * br - z_im[..., None] * bi
    bb_im = z_re[..., None] * bi + z_im[..., None] * br
    cr, ci = c_re.astype(f32), c_im.astype(f32)
    ch = CHUNK if L % CHUNK == 0 else L
    nc = L // ch
    uf = u.astype(f32).reshape(Bsz, nc, ch, SSM_GROUPS, SSM_GROUP).swapaxes(0, 1)

    def chunk_step(carry, u_c):
        hr, hi = carry
        xr = jnp.einsum('bcgi,gpi->bcgp', u_c, bb_re)
        xi = jnp.einsum('bcgi,gpi->bcgp', u_c, bb_im)
        xr = xr.at[:, 0].add(abar_re * hr - abar_im * hi)
        xi = xi.at[:, 0].add(abar_re * hi + abar_im * hr)
        a_r = jnp.broadcast_to(abar_re, xr.shape)
        a_i = jnp.broadcast_to(abar_im, xr.shape)
        _, _, sr, si = lax.associative_scan(complex_combine, (a_r, a_i, xr, xi), axis=1)
        y = jnp.einsum('bcgp,gip->bcgi', sr, cr) - jnp.einsum('bcgp,gip->bcgi', si, ci)
        return (sr[:, -1], si[:, -1]), y

    (hr, hi), ys = lax.scan(chunk_step, (h0_re.astype(f32), h0_im.astype(f32)), uf)
    y = ys.swapaxes(0, 1).reshape(Bsz, L, SSM_WIDTH) + d.astype(f32) * u.astype(f32)
    return y, hr, hi


def rg_lru(xc, h0, w_r, b_r, w_i, b_i, lam):
    f32 = jnp.float32
    Bsz, L, _ = xc.shape
    xf = xc.astype(f32)
    xb = xf.reshape(Bsz, L, LRU_BLOCKS, LRU_BLOCK)
    r = jax.nn.sigmoid(jnp.einsum('blhi,hij->blhj', xb, w_r.astype(f32)).reshape(Bsz, L, LRU_WIDTH) + b_r)
    ig = jax.nn.sigmoid(jnp.einsum('blhi,hij->blhj', xb, w_i.astype(f32)).reshape(Bsz, L, LRU_WIDTH) + b_i)
    log_a = -LRU_C * r * jax.nn.softplus(-lam.astype(f32))
    a = jnp.exp(log_a)
    bterm = jnp.sqrt(-jnp.expm1(2.0 * log_a)) * ig * xf
    bterm = bterm.at[:, 0].add(a[:, 0] * h0.astype(f32))
    _, h = lax.associative_scan(real_combine, (a, bterm), axis=1)
    return h, h[:, -1]


def peer(x, w_q, sub_keys, u_tab, v_tab):
    f32 = jnp.float32
    Bsz, L, D = x.shape
    T = Bsz * L
    nb = -(-T // PEER_BLOCK)
    xt = jnp.pad(x.reshape(T, D), ((0, nb * PEER_BLOCK - T), (0, 0))).reshape(nb, PEER_BLOCK, D)
    sk = sub_keys.astype(f32)

    def block(xb):
        q = (xb @ w_q).astype(f32).reshape(PEER_BLOCK, PEER_HEADS, 2, PEER_HALF)
        s = jnp.einsum('thpd,pnd->thpn', q, sk)
        s1, i1 = lax.top_k(s[:, :, 0], PEER_TOPK)
        s2, i2 = lax.top_k(s[:, :, 1], PEER_TOPK)
        cand = (s1[..., :, None] + s2[..., None, :]).reshape(PEER_BLOCK, PEER_HEADS, PEER_TOPK * PEER_TOPK)
        top, cidx = lax.top_k(cand, PEER_TOPK)
        e = (jnp.take_along_axis(i1, cidx // PEER_TOPK, axis=-1) * N_KEYS
             + jnp.take_along_axis(i2, cidx % PEER_TOPK, axis=-1))
        g = jax.nn.softmax(top, axis=-1)
        hid = jnp.einsum('thkd,td->thk', u_tab[e], xb).astype(f32)
        act = (g * jax.nn.gelu(hid)).astype(xb.dtype)
        return jnp.einsum('thk,thkd->td', act, v_tab[e])

    out = lax.map(block, xt).reshape(nb * PEER_BLOCK, D)[:T]
    return out.reshape(Bsz, L, D)


def even_mixer(h, k_past, v_past, s_re, s_im, p, e):
    Bsz, L, _ = h.shape
    proj = h @ p['w_in_e'][e]
    q, k, v, u = jnp.split(proj, [SB_WIDTH, 2 * SB_WIDTH, 3 * SB_WIDTH], axis=-1)
    shp = (Bsz, L, SB_HEADS, SB_HEAD_DIM)
    q, k, v = q.reshape(shp), k.reshape(shp), v.reshape(shp)
    att = stick_breaking_attention(q, k, v, k_past, v_past).reshape(Bsz, L, SB_WIDTH)
    y, s_re, s_im = s5_ssm(u, s_re, s_im, p['ssm_a_re'][e], p['ssm_a_im'][e], p['ssm_log_dt'][e],
                           p['ssm_b_re'][e], p['ssm_b_im'][e], p['ssm_c_re'][e], p['ssm_c_im'][e], p['ssm_d'][e])
    g = jax.nn.gelu(y)
    ssm_out = (g * jax.nn.sigmoid(g @ p['ssm_w_glu'][e].astype(jnp.float32))).astype(h.dtype)
    out = jnp.concatenate([att, ssm_out], axis=-1) @ p['w_out_e'][e]
    return out, k, v, s_re, s_im


def odd_mixer(h, lru_h, lru_hist, d_hist, p, o):
    proj = h @ p['w_in_o'][o]
    xr, gate, dv, dg = jnp.split(proj, [LRU_WIDTH, 2 * LRU_WIDTH, 2 * LRU_WIDTH + CONF_WIDTH], axis=-1)
    xc, new_lru_hist = causal_dwconv(xr, lru_hist, p['lru_conv_w'][o], p['lru_conv_b'][o])
    hs, h_last = rg_lru(xc, lru_h, p['lru_w_r'][o], p['lru_b_r'][o], p['lru_w_i'][o], p['lru_b_i'][o], p['lru_lam'][o])
    lru_out = (hs * jax.nn.gelu(gate.astype(jnp.float32))).astype(h.dtype)
    glu = dv * jax.nn.sigmoid(dg)
    dc, new_d_hist = causal_dwconv(glu, d_hist, p['dconv_w'][o], p['dconv_b'][o])
    conf_out = jax.nn.silu(layer_norm(dc, p['dnorm_g'][o], p['dnorm_b'][o]))
    out = jnp.concatenate([lru_out, conf_out], axis=-1) @ p['w_out_o'][o]
    return out, h_last, new_lru_hist, new_d_hist


def trunk(x, c, k_past, v_past, ssm_re, ssm_im, lru_h, lru_hist, d_hist, p):
    ks, vs, sres, sims, lhs, lcs, dcs = [], [], [], [], [], [], []
    for l in range(DEPTH):
        shift, scale, gate = modulation(c, p['ada_w'][l, 0], p['ada_b'][l, 0])
        h = x * (1.0 + scale) + shift
        if l % 2 == 0:
            e = l // 2
            out, k, v, s_re, s_im = even_mixer(h, k_past[e], v_past[e], ssm_re[e], ssm_im[e], p, e)
            ks.append(k)
            vs.append(v)
            sres.append(s_re)
            sims.append(s_im)
        else:
            o = l // 2
            out, hl, lc, dc = odd_mixer(h, lru_h[o], lru_hist[o], d_hist[o], p, o)
            lhs.append(hl)
            lcs.append(lc)
            dcs.append(dc)
        x = layer_norm(ALPHA * x + (1.0 + gate) * out, p['ln_g'][l, 0], p['ln_b'][l, 0])
        shift, scale, gate = modulation(c, p['ada_w'][l, 1], p['ada_b'][l, 1])
        h = x * (1.0 + scale) + shift
        out = peer(h, p['peer_wq'][l], p['peer_keys'][l], p['peer_u'][l], p['peer_v'][l])
        x = layer_norm(ALPHA * x + (1.0 + gate) * out, p['ln_g'][l, 1], p['ln_b'][l, 1])
    return (x, jnp.stack(ks), jnp.stack(vs), jnp.stack(sres), jnp.stack(sims),
            jnp.stack(lhs), jnp.stack(lcs), jnp.stack(dcs))


def setup_inputs(seed: int = 0) -> dict:
    key = jax.random.key(seed)
    keys = iter(jax.random.split(key, 64))
    f = jnp.float32
    D = D_MODEL

    def nrm(shape, scale):
        return jax.random.normal(next(keys), shape, f) * scale

    in_e = 3 * SB_WIDTH + SSM_WIDTH
    in_o = 2 * LRU_WIDTH + 2 * CONF_WIDTH
    col_scale_e = jnp.concatenate([jnp.ones((2 * SB_WIDTH,), f), jnp.full((SB_WIDTH,), BETA, f),
                                   jnp.ones((SSM_WIDTH,), f)])
    a0 = jax.random.uniform(next(keys), (N_ODD, LRU_WIDTH), f, 0.9, 0.999)
    base = a0 ** (1.0 / LRU_C)
    lru_lam = jnp.log(base) - jnp.log1p(-base)
    log_dt = jax.random.uniform(next(keys), (N_EVEN, SSM_GROUPS), f, math.log(1e-3), math.log(1e-1))
    a_im = jnp.broadcast_to(math.pi * jnp.arange(SSM_STATE, dtype=f), (N_EVEN, SSM_GROUPS, SSM_STATE))
    return {
        'x_prompt': nrm((BATCH, SEQ, D), 1.0),
        'x_sample': nrm((DEC_BATCH, DEC_SEQ, D), 1.0),
        'c_prompt': nrm((BATCH, D), 1.0),
        'c_sample': nrm((DEC_BATCH, D), 1.0),
        'cache_k': nrm((N_EVEN, DEC_BATCH, PAST_LEN, SB_HEADS, SB_HEAD_DIM), 1.0),
        'cache_v': nrm((N_EVEN, DEC_BATCH, PAST_LEN, SB_HEADS, SB_HEAD_DIM), BETA),
        'state_ssm_re': nrm((N_EVEN, DEC_BATCH, SSM_GROUPS, SSM_STATE), 0.1),
        'state_ssm_im': nrm((N_EVEN, DEC_BATCH, SSM_GROUPS, SSM_STATE), 0.1),
        'state_lru': nrm((N_ODD, DEC_BATCH, LRU_WIDTH), 0.5),
        'state_lru_conv': nrm((N_ODD, DEC_BATCH, LRU_CONV - 1, LRU_WIDTH), 1.0),
        'state_dconv': nrm((N_ODD, DEC_BATCH, CONF_CONV - 1, CONF_WIDTH), 0.5),
        'ada_w': nrm((DEPTH, 2, D, 3 * D), 0.1 * D ** -0.5),
        'ada_b': nrm((DEPTH, 2, 3 * D), 0.01),
        'ln_g': 1.0 + nrm((DEPTH, 2, D), 0.05),
        'ln_b': nrm((DEPTH, 2, D), 0.01),
        'w_in_e': nrm((N_EVEN, D, in_e), D ** -0.5) * col_scale_e,
        'w_out_e': nrm((N_EVEN, SB_WIDTH + SSM_WIDTH, D), BETA * (SB_WIDTH + SSM_WIDTH) ** -0.5),
        'ssm_a_re': -0.5 + nrm((N_EVEN, SSM_GROUPS, SSM_STATE), 0.01),
        'ssm_a_im': a_im + nrm((N_EVEN, SSM_GROUPS, SSM_STATE), 0.01),
        'ssm_log_dt': log_dt,
        'ssm_b_re': nrm((N_EVEN, SSM_GROUPS, SSM_STATE, SSM_GROUP), (2.0 * SSM_GROUP) ** -0.5),
        'ssm_b_im': nrm((N_EVEN, SSM_GROUPS, SSM_STATE, SSM_GROUP), (2.0 * SSM_GROUP) ** -0.5),
        'ssm_c_re': nrm((N_EVEN, SSM_GROUPS, SSM_GROUP, SSM_STATE), (2.0 * SSM_STATE) ** -0.5),
        'ssm_c_im': nrm((N_EVEN, SSM_GROUPS, SSM_GROUP, SSM_STATE), (2.0 * SSM_STATE) ** -0.5),
        'ssm_d': nrm((N_EVEN, SSM_WIDTH), 0.5),
        'ssm_w_glu': nrm((N_EVEN, SSM_WIDTH, SSM_WIDTH), SSM_WIDTH ** -0.5),
        'w_in_o': nrm((N_ODD, D, in_o), D ** -0.5),
        'w_out_o': nrm((N_ODD, LRU_WIDTH + CONF_WIDTH, D), BETA * (LRU_WIDTH + CONF_WIDTH) ** -0.5),
        'lru_conv_w': nrm((N_ODD, LRU_CONV, LRU_WIDTH), LRU_CONV ** -0.5),
        'lru_conv_b': nrm((N_ODD, LRU_WIDTH), 0.01),
        'lru_w_r': nrm((N_ODD, LRU_BLOCKS, LRU_BLOCK, LRU_BLOCK), LRU_BLOCK ** -0.5),
        'lru_b_r': nrm((N_ODD, LRU_WIDTH), 0.01),
        'lru_w_i': nrm((N_ODD, LRU_BLOCKS, LRU_BLOCK, LRU_BLOCK), LRU_BLOCK ** -0.5),
        'lru_b_i': nrm((N_ODD, LRU_WIDTH), 0.01),
        'lru_lam': lru_lam,
        'dconv_w': nrm((N_ODD, CONF_CONV, CONF_WIDTH), CONF_CONV ** -0.5),
        'dconv_b': nrm((N_ODD, CONF_WIDTH), 0.01),
        'dnorm_g': 1.0 + nrm((N_ODD, CONF_WIDTH), 0.05),
        'dnorm_b': nrm((N_ODD, CONF_WIDTH), 0.01),
        'peer_wq': nrm((DEPTH, D, PEER_HEADS * PEER_QDIM), D ** -0.5),
        'peer_keys': nrm((DEPTH, 2, N_KEYS, PEER_HALF), PEER_HALF ** -0.5),
        'peer_u': nrm((DEPTH, N_EXPERTS, D), D ** -0.5),
        'peer_v': nrm((DEPTH, N_EXPERTS, D), BETA * PEER_HEADS ** -0.5),
    }


def reference(x_prompt, x_sample, c_prompt, c_sample, cache_k, cache_v, state_ssm_re, state_ssm_im,
              state_lru, state_lru_conv, state_dconv,
              ada_w, ada_b, ln_g, ln_b,
              w_in_e, w_out_e, ssm_a_re, ssm_a_im, ssm_log_dt, ssm_b_re, ssm_b_im, ssm_c_re, ssm_c_im,
              ssm_d, ssm_w_glu,
              w_in_o, w_out_o, lru_conv_w, lru_conv_b, lru_w_r, lru_b_r, lru_w_i, lru_b_i, lru_lam,
              dconv_w, dconv_b, dnorm_g, dnorm_b,
              peer_wq, peer_keys, peer_u, peer_v):
    p = dict(ada_w=ada_w, ada_b=ada_b, ln_g=ln_g, ln_b=ln_b,
             w_in_e=w_in_e, w_out_e=w_out_e, ssm_a_re=ssm_a_re, ssm_a_im=ssm_a_im, ssm_log_dt=ssm_log_dt,
             ssm_b_re=ssm_b_re, ssm_b_im=ssm_b_im, ssm_c_re=ssm_c_re, ssm_c_im=ssm_c_im, ssm_d=ssm_d,
             ssm_w_glu=ssm_w_glu, w_in_o=w_in_o, w_out_o=w_out_o, lru_conv_w=lru_conv_w, lru_conv_b=lru_conv_b,
             lru_w_r=lru_w_r, lru_b_r=lru_b_r, lru_w_i=lru_w_i, lru_b_i=lru_b_i, lru_lam=lru_lam,
             dconv_w=dconv_w, dconv_b=dconv_b, dnorm_g=dnorm_g, dnorm_b=dnorm_b,
             peer_wq=peer_wq, peer_keys=peer_keys, peer_u=peer_u, peer_v=peer_v)
    f32 = jnp.float32
    kv0 = jnp.zeros((N_EVEN, BATCH, 0, SB_HEADS, SB_HEAD_DIM), x_prompt.dtype)
    ssm0 = jnp.zeros((N_EVEN, BATCH, SSM_GROUPS, SSM_STATE), f32)
    lru0 = jnp.zeros((N_ODD, BATCH, LRU_WIDTH), f32)
    lconv0 = jnp.zeros((N_ODD, BATCH, LRU_CONV - 1, LRU_WIDTH), x_prompt.dtype)
    dconv0 = jnp.zeros((N_ODD, BATCH, CONF_CONV - 1, CONF_WIDTH), x_prompt.dtype)
    (y_prompt, k_prompt, v_prompt, ssm_re_prompt, ssm_im_prompt, lru_prompt, lru_conv_prompt,
     dconv_prompt) = trunk(x_prompt, c_prompt, kv0, kv0, ssm0, ssm0, lru0, lconv0, dconv0, p)
    (y_sample, k_sample, v_sample, ssm_re_sample, ssm_im_sample, lru_sample, lru_conv_sample,
     dconv_sample) = trunk(x_sample, c_sample, cache_k, cache_v, state_ssm_re, state_ssm_im, state_lru,
                           state_lru_conv, state_dconv, p)
    return (y_prompt, y_sample, k_prompt, v_prompt, k_sample, v_sample,
            ssm_re_prompt, ssm_im_prompt, ssm_re_sample, ssm_im_sample,
            lru_prompt, lru_sample, lru_conv_prompt, lru_conv_sample, dconv_prompt, dconv_sample)
```

```python
import functools
import math

import jax
import jax.numpy as jnp
from jax import lax
from jax.experimental import pallas as pl
from jax.experimental.pallas import tpu as pltpu

F32 = jnp.float32
BF16 = jnp.bfloat16
HIGHEST = lax.Precision.HIGHEST

LANES = 128
SUBLANES = 8
VMEM_LIMIT_BYTES = 56 * 1024 * 1024

LN_EPS = 1e-5
LRU_C = 8.0
PEER_TOPK = 16
SSM_CHUNK = 16


def _cparams(sem):
    return pltpu.CompilerParams(dimension_semantics=sem, vmem_limit_bytes=VMEM_LIMIT_BYTES)


def _tile(n, pref, mult=SUBLANES):
    if n <= pref:
        return n
    t = (pref // mult) * mult
    while n % t:
        t -= mult
    return t


def _gelu(x):
    return jax.nn.gelu(x, approximate=True)


def _sigmoid(x):
    return jax.nn.sigmoid(x)


def _ada_kernel(c_ref, w_ref, b_ref, o_ref):
    c = c_ref[...]
    s = (c * _sigmoid(c)).astype(BF16)
    o_ref[0] = jnp.dot(s, w_ref[0].astype(BF16), preferred_element_type=F32) + b_ref[0]


def ada_modulation(c_all, ada_w, ada_b):
    nl, d, n3 = ada_w.shape
    r = c_all.shape[0]
    tn = _tile(n3, 768, LANES)
    return pl.pallas_call(
        _ada_kernel,
        grid=(nl, n3 // tn),
        in_specs=[pl.BlockSpec((r, d), lambda l, j: (0, 0)),
                  pl.BlockSpec((1, d, tn), lambda l, j: (l, 0, j)),
                  pl.BlockSpec((1, 1, tn), lambda l, j: (l, 0, j))],
        out_specs=pl.BlockSpec((1, r, tn), lambda l, j: (l, 0, j)),
        out_shape=jax.ShapeDtypeStruct((nl, r, n3), F32),
        compiler_params=_cparams(("arbitrary", "arbitrary")),
        name="ada_modulation",
    )(c_all, ada_w, ada_b)


def _modulate_kernel(x_ref, sc_ref, sh_ref, o_ref):
    o_ref[0] = (x_ref[0] * (1.0 + sc_ref[0]) + sh_ref[0]).astype(o_ref.dtype)


def modulate(x, scale, shift):
    b, l, d = x.shape
    tl = _tile(l, 512)
    vec = pl.BlockSpec((1, 1, d), lambda i, j: (i, 0, 0))
    blk = pl.BlockSpec((1, tl, d), lambda i, j: (i, j, 0))
    return pl.pallas_call(
        _modulate_kernel, grid=(b, l // tl),
        in_specs=[blk, vec, vec], out_specs=blk,
        out_shape=jax.ShapeDtypeStruct((b, l, d), BF16),
        compiler_params=_cparams(("arbitrary", "arbitrary")),
        name="modulate",
    )(x, scale, shift)


def _res_ln_kernel(*refs, alpha, with_next):
    if with_next:
        x_ref, o_ref, gate_ref, g_ref, b_ref, sc_ref, sh_ref, xn_ref, hn_ref = refs
    else:
        x_ref, o_ref, gate_ref, g_ref, b_ref, xn_ref = refs
    y = alpha * x_ref[0] + (1.0 + gate_ref[0]) * o_ref[0]
    mu = jnp.mean(y, axis=-1, keepdims=True)
    yc = y - mu
    var = jnp.mean(yc * yc, axis=-1, keepdims=True)
    xn = yc * lax.rsqrt(var + LN_EPS) * g_ref[...] + b_ref[...]
    xn_ref[0] = xn
    if with_next:
        hn_ref[0] = (xn * (1.0 + sc_ref[0]) + sh_ref[0]).astype(hn_ref.dtype)


def res_ln(x, out, gate, g, b, alpha, nxt=None):
    bsz, l, d = x.shape
    tl = _tile(l, 256)
    vec = pl.BlockSpec((1, 1, d), lambda i, j: (i, 0, 0))
    par = pl.BlockSpec((1, d), lambda i, j: (0, 0))
    blk = pl.BlockSpec((1, tl, d), lambda i, j: (i, j, 0))
    ins = [x, out, gate, g.reshape(1, d), b.reshape(1, d)]
    in_specs = [blk, blk, vec, par, par]
    out_shape = [jax.ShapeDtypeStruct((bsz, l, d), F32)]
    out_specs = [blk]
    if nxt is not None:
        ins += [nxt[0], nxt[1]]
        in_specs += [vec, vec]
        out_shape.append(jax.ShapeDtypeStruct((bsz, l, d), BF16))
        out_specs.append(blk)
    res = pl.pallas_call(
        functools.partial(_res_ln_kernel, alpha=alpha, with_next=nxt is not None),
        grid=(bsz, l // tl), in_specs=in_specs, out_specs=out_specs, out_shape=out_shape,
        compiler_params=_cparams(("arbitrary", "arbitrary")),
        name="res_ln",
    )(*ins)
    return (res[0], res[1]) if nxt is not None else (res[0], None)


def _mm_kernel(*refs, k_sizes):
    n_x = len(k_sizes)
    w_ref, o_ref = refs[n_x], refs[n_x + 1]
    acc = None
    off = 0
    for x_ref, ks in zip(refs[:n_x], k_sizes):
        part = jnp.dot(x_ref[...], w_ref[off:off + ks, :], preferred_element_type=F32)
        acc = part if acc is None else acc + part
        off += ks
    o_ref[...] = acc.astype(o_ref.dtype)


def matmul(xs, w, out_dtype=F32, tm_pref=1024, tn_pref=512):
    m = xs[0].shape[0]
    k, n = w.shape
    k_sizes = tuple(x.shape[1] for x in xs)
    assert sum(k_sizes) == k
    tm = _tile(m, tm_pref)
    tn = _tile(n, tn_pref, LANES)
    in_specs = [pl.BlockSpec((tm, ks), lambda i, j: (i, 0)) for ks in k_sizes]
    in_specs.append(pl.BlockSpec((k, tn), lambda i, j: (0, j)))
    return pl.pallas_call(
        functools.partial(_mm_kernel, k_sizes=k_sizes),
        grid=(m // tm, n // tn), in_specs=in_specs,
        out_specs=pl.BlockSpec((tm, tn), lambda i, j: (i, j)),
        out_shape=jax.ShapeDtypeStruct((m, n), out_dtype),
        compiler_params=_cparams(("arbitrary", "arbitrary")),
        name="matmul",
    )(*xs, w)


def _sb_attn_kernel(*refs, qb, kb_past, n_past, scale):
    if n_past:
        q_ref, k_ref, v_ref, pk_ref, pv_ref, tri_ref, o_ref = refs
    else:
        q_ref, k_ref, v_ref, tri_ref, o_ref = refs
    qi = pl.program_id(2)
    q = q_ref[0].astype(BF16)
    dh = q.shape[-1]

    def block(kblk, vblk, carry, acc, mask, tri):
        z = lax.dot_general(q, kblk.astype(BF16), (((1,), (1,)), ((), ())),
                            preferred_element_type=F32) * scale
        soft = jnp.log1p(jnp.exp(-jnp.abs(z)))
        log_sig = jnp.minimum(z, 0.0) - soft
        log_keep = log_sig - z
        if mask is not None:
            log_keep = jnp.where(mask, log_keep, 0.0)
        hi = log_keep.astype(BF16)
        lo = (log_keep - hi.astype(F32)).astype(BF16)
        later = (jnp.dot(hi, tri, preferred_element_type=F32)
                 + jnp.dot(lo, tri, preferred_element_type=F32))
        w = jnp.exp(log_sig + later + carry)
        if mask is not None:
            w = jnp.where(mask, w, 0.0)
        acc = acc + jnp.dot(w.astype(BF16), vblk.astype(BF16), preferred_element_type=F32)
        carry = carry + jnp.sum(log_keep, axis=1, keepdims=True)
        return carry, acc

    row = lax.broadcasted_iota(jnp.int32, (qb, qb), 0)
    col = lax.broadcasted_iota(jnp.int32, (qb, qb), 1)
    start = pl.multiple_of(qi * qb, qb)
    carry0 = jnp.zeros((qb, 1), F32)
    acc0 = jnp.zeros((qb, dh), F32)
    tri_q = tri_ref[0:qb, 0:qb]
    carry, acc = block(k_ref[0, pl.ds(start, qb), :], v_ref[0, pl.ds(start, qb), :],
                       carry0, acc0, col < row, tri_q)

    def new_body(it, ca):
        j = qi - 1 - it
        s = pl.multiple_of(j * qb, qb)
        return block(k_ref[0, pl.ds(s, qb), :], v_ref[0, pl.ds(s, qb), :], ca[0], ca[1], None, tri_q)

    carry, acc = lax.fori_loop(0, qi, new_body, (carry, acc))

    if n_past:
        tri_p = tri_ref[0:kb_past, 0:kb_past]

        def past_body(it, ca):
            j = n_past - 1 - it
            s = pl.multiple_of(j * kb_past, kb_past)
            return block(pk_ref[0, pl.ds(s, kb_past), :], pv_ref[0, pl.ds(s, kb_past), :],
                         ca[0], ca[1], None, tri_p)

        carry, acc = lax.fori_loop(0, n_past, past_body, (carry, acc))
    o_ref[0] = acc.astype(o_ref.dtype)


def sb_attention(proj, n_heads, dh, k_past=None, v_past=None):
    b, l, _ = proj.shape
    qb = _tile(l, 256)
    n_past, kb_past = 0, qb
    if k_past is not None and k_past.shape[1] > 0:
        p = k_past.shape[1]
        kb_past = _tile(p, 256)
        n_past = p // kb_past
    tb = max(qb, kb_past)
    tri = (lax.broadcasted_iota(jnp.int32, (tb, tb), 0)
           > lax.broadcasted_iota(jnp.int32, (tb, tb), 1)).astype(BF16)
    h = n_heads
    in_specs = [pl.BlockSpec((1, qb, dh), lambda bi, hi, qi: (bi, qi, hi)),
                pl.BlockSpec((1, l, dh), lambda bi, hi, qi: (bi, 0, h + hi)),
                pl.BlockSpec((1, l, dh), lambda bi, hi, qi: (bi, 0, 2 * h + hi))]
    ins = [proj, proj, proj]
    if n_past:
        p = k_past.shape[1]
        in_specs += [pl.BlockSpec((1, p, dh), lambda bi, hi, qi: (bi, 0, hi))] * 2
        ins += [k_past, v_past]
    in_specs.append(pl.BlockSpec((tb, tb), lambda bi, hi, qi: (0, 0)))
    ins.append(tri)
    return pl.pallas_call(
        functools.partial(_sb_attn_kernel, qb=qb, kb_past=kb_past, n_past=n_past, scale=dh ** -0.5),
        grid=(b, h, l // qb), in_specs=in_specs,
        out_specs=pl.BlockSpec((1, qb, dh), lambda bi, hi, qi: (bi, qi, hi)),
        out_shape=jax.ShapeDtypeStruct((b, l, h * dh), BF16),
        compiler_params=_cparams(("arbitrary", "arbitrary", "arbitrary")),
        name="sb_attention",
    )(*ins)


def _s5_inject_kernel(u_ref, bc_ref, re_ref, im_ref):
    half = u_ref.shape[2]
    inj = (jnp.dot(u_ref[0], bc_ref[0, 0:half, :], precision=HIGHEST, preferred_element_type=F32)
           + jnp.dot(u_ref[1], bc_ref[0, half:, :], precision=HIGHEST, preferred_element_type=F32))
    re_ref[...] = inj[:, 0:LANES]
    im_ref[...] = inj[:, LANES:]


def _s5_scan_kernel(ire_ref, iim_ref, h0r_ref, h0i_ref, ar_ref, ai_ref,
                    sre_ref, sim_ref, fre_ref, fim_ref):
    n_chunks = ire_ref.shape[0]
    ar = ar_ref[...]
    ai = ai_ref[...]

    def body(k, h):
        hr, hi = h
        sre_ref[k] = hr
        sim_ref[k] = hi
        return (ar * hr - ai * hi + ire_ref[k], ar * hi + ai * hr + iim_ref[k])

    hr, hi = lax.fori_loop(0, n_chunks, body, (h0r_ref[...], h0i_ref[...]))
    fre_ref[...] = hr
    fim_ref[...] = hi


def _s5_output_kernel(u_ref, m_ref, sre_ref, sim_ref, cc_ref, y_ref):
    y_in = jnp.concatenate(
        [jnp.dot(u_ref[0], m_ref[0], precision=HIGHEST, preferred_element_type=F32),
         jnp.dot(u_ref[1], m_ref[1], precision=HIGHEST, preferred_element_type=F32)], axis=1)
    y_ref[...] = (y_in
                  + jnp.dot(sre_ref[...], cc_ref[0, 0:LANES, :], precision=HIGHEST,
                            preferred_element_type=F32)
                  + jnp.dot(sim_ref[...], cc_ref[0, LANES:, :], precision=HIGHEST,
                            preferred_element_type=F32))


def s5_matrices(a_re, a_im, log_dt, b_re, b_im, c_re, c_im):
    g, p = a_re.shape
    n_in = b_re.shape[-1]
    ch = SSM_CHUNK
    dt = jnp.exp(log_dt)[:, None]
    mag = jnp.exp(a_re * dt)
    abr, abi = mag * jnp.cos(a_im * dt), mag * jnp.sin(a_im * dt)
    den = a_re * a_re + a_im * a_im
    nr, ni = abr - 1.0, abi
    z_re, z_im = (nr * a_re + ni * a_im) / den, (ni * a_re - nr * a_im) / den
    bb_re = z_re[..., None] * b_re - z_im[..., None] * b_im
    bb_im = z_re[..., None] * b_im + z_im[..., None] * b_re
    steps = jnp.arange(ch + 1, dtype=F32)[:, None, None]
    pmag = jnp.exp(steps * (a_re * dt)[None])
    pw_re = pmag * jnp.cos(steps * (a_im * dt)[None])
    pw_im = pmag * jnp.sin(steps * (a_im * dt)[None])
    e_re = c_re[None] * pw_re[:, :, None, :] - c_im[None] * pw_im[:, :, None, :]
    e_im = c_re[None] * pw_im[:, :, None, :] + c_im[None] * pw_re[:, :, None, :]
    kern = (jnp.einsum('ngop,gpi->ngoi', e_re[:ch], bb_re, precision=HIGHEST)
            - jnp.einsum('ngop,gpi->ngoi', e_im[:ch], bb_im, precision=HIGHEST))
    tj = jnp.arange(ch)[None, :] - jnp.arange(ch)[:, None]
    kg = kern[jnp.clip(tj, 0, ch - 1)]
    kg = jnp.where((tj >= 0)[:, :, None, None, None], kg, 0.0)
    m_mat = kg.transpose(2, 0, 4, 1, 3).reshape(g, ch * n_in, ch * n_in)
    rp_re, rp_im = pw_re[ch - 1::-1][:ch], pw_im[ch - 1::-1][:ch]
    bc_re = rp_re[:, :, :, None] * bb_re[None] - rp_im[:, :, :, None] * bb_im[None]
    bc_im = rp_re[:, :, :, None] * bb_im[None] + rp_im[:, :, :, None] * bb_re[None]
    bc_re = bc_re.transpose(1, 0, 3, 2).reshape(g, ch * n_in, p)
    bc_im = bc_im.transpose(1, 0, 3, 2).reshape(g, ch * n_in, p)
    zeros = jnp.zeros_like(bc_re[0::2])
    top = jnp.concatenate([bc_re[0::2], zeros, bc_im[0::2], zeros], axis=2)
    bot = jnp.concatenate([zeros, bc_re[1::2], zeros, bc_im[1::2]], axis=2)
    bc_pair = jnp.concatenate([top, bot], axis=1)
    cr = e_re[1:].transpose(1, 3, 0, 2).reshape(g, p, ch * n_in)
    ci = -e_im[1:].transpose(1, 3, 0, 2).reshape(g, p, ch * n_in)
    zc = jnp.zeros_like(cr[0::2])
    cc_pair = jnp.concatenate([
        jnp.concatenate([cr[0::2], zc], axis=2),
        jnp.concatenate([zc, cr[1::2]], axis=2),
        jnp.concatenate([ci[0::2], zc], axis=2),
        jnp.concatenate([zc, ci[1::2]], axis=2)], axis=1)
    a_pow = (pw_re[ch].reshape(1, g * p), pw_im[ch].reshape(1, g * p))
    return m_mat, bc_pair, cc_pair, a_pow


def s5_ssm(u, h0_re, h0_im, mats):
    m_mat, bc_pair, cc_pair, (apr, api) = mats
    b, l, w = u.shape
    g = m_mat.shape[0]
    n_in = w // g
    p = h0_re.shape[-1]
    ch = SSM_CHUNK
    assert l % ch == 0 and 2 * p == LANES and g % 2 == 0
    k = l // ch
    m = k * b
    cw = ch * n_in
    uf = u.reshape(b, k, ch, g, n_in).transpose(3, 1, 0, 2, 4).reshape(g, m, cw)
    gp = g // 2
    sw = g * p
    inj_re, inj_im = pl.pallas_call(
        _s5_inject_kernel, grid=(gp,),
        in_specs=[pl.BlockSpec((2, m, cw), lambda i: (i, 0, 0)),
                  pl.BlockSpec((1, 2 * cw, 4 * p), lambda i: (i, 0, 0))],
        out_specs=[pl.BlockSpec((m, LANES), lambda i: (0, i))] * 2,
        out_shape=[jax.ShapeDtypeStruct((m, sw), F32)] * 2,
        compiler_params=_cparams(("arbitrary",)),
        name="s5_inject",
    )(uf, bc_pair)
    tw = _tile(sw, 512, LANES)
    seq = pl.BlockSpec((k, b, tw), lambda i: (0, 0, i))
    st = pl.BlockSpec((b, tw), lambda i: (0, i))
    par = pl.BlockSpec((1, tw), lambda i: (0, i))
    s_re, s_im, f_re, f_im = pl.pallas_call(
        _s5_scan_kernel, grid=(sw // tw,),
        in_specs=[seq, seq, st, st, par, par],
        out_specs=[seq, seq, st, st],
        out_shape=[jax.ShapeDtypeStruct((k, b, sw), F32)] * 2 + [jax.ShapeDtypeStruct((b, sw), F32)] * 2,
        compiler_params=_cparams(("arbitrary",)),
        name="s5_scan",
    )(inj_re.reshape(k, b, sw), inj_im.reshape(k, b, sw),
      h0_re.reshape(b, sw), h0_im.reshape(b, sw), apr, api)
    y = pl.pallas_call(
        _s5_output_kernel, grid=(gp,),
        in_specs=[pl.BlockSpec((2, m, cw), lambda i: (i, 0, 0)),
                  pl.BlockSpec((2, cw, cw), lambda i: (i, 0, 0)),
                  pl.BlockSpec((m, LANES), lambda i: (0, i)),
                  pl.BlockSpec((m, LANES), lambda i: (0, i)),
                  pl.BlockSpec((1, 4 * p, 2 * cw), lambda i: (i, 0, 0))],
        out_specs=pl.BlockSpec((m, 2 * cw), lambda i: (0, i)),
        out_shape=jax.ShapeDtypeStruct((m, g * cw), F32),
        compiler_params=_cparams(("arbitrary",)),
        name="s5_output",
    )(uf, m_mat, s_re.reshape(m, sw), s_im.reshape(m, sw), cc_pair)
    y = y.reshape(k, b, g, ch, n_in).transpose(1, 0, 3, 2, 4).reshape(b, l, w)
    return y, f_re.reshape(b, g, p), f_im.reshape(b, g, p)


def _ssm_glu_kernel(y_ref, u_ref, d_ref, w_ref, o_ref):
    gact = _gelu(y_ref[...] + d_ref[...] * u_ref[...])
    gate = _sigmoid(jnp.dot(gact.astype(BF16), w_ref[...], preferred_element_type=F32))
    o_ref[...] = (gact * gate).astype(o_ref.dtype)


def ssm_glu(y2d, proj2d, u_col_block, d, w_glu):
    m, w = y2d.shape
    tm = _tile(m, 512)
    return pl.pallas_call(
        _ssm_glu_kernel, grid=(m // tm,),
        in_specs=[pl.BlockSpec((tm, w), lambda i: (i, 0)),
                  pl.BlockSpec((tm, w), lambda i: (i, u_col_block)),
                  pl.BlockSpec((1, w), lambda i: (0, 0)),
                  pl.BlockSpec((w, w), lambda i: (0, 0))],
        out_specs=pl.BlockSpec((tm, w), lambda i: (i, 0)),
        out_shape=jax.ShapeDtypeStruct((m, w), BF16),
        compiler_params=_cparams(("arbitrary",)),
        name="ssm_glu",
    )(y2d, proj2d, d.reshape(1, w), w_glu)


def _lru_kernel(x_ref, gate_ref, hist_ref, h0_ref, cw_ref, cb_ref, wr_ref, br_ref, wi_ref, bi_ref,
                lam_ref, o_ref, hl_ref, nh_ref, xbuf, a_buf, b_buf, hcar, *, tt, n_taps):
    ti = pl.program_id(1)
    pad = SUBLANES
    nh = n_taps - 1

    @pl.when(ti == 0)
    def _():
        xbuf[pad - nh:pad, :] = hist_ref[0]
        hcar[...] = h0_ref[0]

    xbuf[pad:pad + tt, :] = x_ref[0]
    xc = cb_ref[...] + cw_ref[0:1, :] * xbuf[pad - nh:pad - nh + tt, :]
    for k in range(1, n_taps):
        xc = xc + cw_ref[k:k + 1, :] * xbuf[pad - nh + k:pad - nh + k + tt, :]
    last = xbuf[pad + tt - nh:pad + tt, :]
    nh_ref[0] = last
    xbuf[pad - nh:pad, :] = last

    nb = wr_ref.shape[0]
    bw = wr_ref.shape[1]
    xcb = xc.astype(BF16)
    r_parts, i_parts = [], []
    for blk in range(nb):
        xs = xcb[:, blk * bw:(blk + 1) * bw]
        r_parts.append(jnp.dot(xs, wr_ref[blk], preferred_element_type=F32))
        i_parts.append(jnp.dot(xs, wi_ref[blk], preferred_element_type=F32))
    r = _sigmoid(jnp.concatenate(r_parts, axis=1) + br_ref[...])
    ig = _sigmoid(jnp.concatenate(i_parts, axis=1) + bi_ref[...])
    nlam = -lam_ref[...]
    softplus = jnp.maximum(nlam, 0.0) + jnp.log1p(jnp.exp(-jnp.abs(nlam)))
    log_a = -LRU_C * r * softplus
    a_buf[...] = jnp.exp(log_a)
    th = jnp.tanh(log_a)
    b_buf[...] = jnp.sqrt(-2.0 * th / (1.0 - th)) * ig * xc

    def body(t, h):
        h = a_buf[pl.ds(t, 1), :] * h + b_buf[pl.ds(t, 1), :]
        b_buf[pl.ds(t, 1), :] = h
        return h

    h = lax.fori_loop(0, tt, body, hcar[...])
    hcar[...] = h
    hl_ref[0] = h
    o_ref[0] = (b_buf[...] * _gelu(gate_ref[0])).astype(o_ref.dtype)


def lru_branch(proj, hist, h0, cw, cb, wr, br, wi, bi, lam):
    b, l, _ = proj.shape
    w = h0.shape[-1]
    n_taps = cw.shape[0]
    tt = _tile(l, 256)
    row = lambda a: a.reshape(1, w)
    par = pl.BlockSpec((1, w), lambda i, j: (0, 0))
    wspec = pl.BlockSpec(wr.shape, lambda i, j: (0, 0, 0))
    return pl.pallas_call(
        functools.partial(_lru_kernel, tt=tt, n_taps=n_taps),
        grid=(b, l // tt),
        in_specs=[pl.BlockSpec((1, tt, w), lambda i, j: (i, j, 0)),
                  pl.BlockSpec((1, tt, w), lambda i, j: (i, j, 1)),
                  pl.BlockSpec((1, n_taps - 1, w), lambda i, j: (i, 0, 0)),
                  pl.BlockSpec((1, 1, w), lambda i, j: (i, 0, 0)),
                  pl.BlockSpec((n_taps, w), lambda i, j: (0, 0)),
                  par, wspec, par, wspec, par, par],
        out_specs=[pl.BlockSpec((1, tt, w), lambda i, j: (i, j, 0)),
                   pl.BlockSpec((1, 1, w), lambda i, j: (i, 0, 0)),
                   pl.BlockSpec((1, n_taps - 1, w), lambda i, j: (i, 0, 0))],
        out_shape=[jax.ShapeDtypeStruct((b, l, w), BF16),
                   jax.ShapeDtypeStruct((b, 1, w), F32),
                   jax.ShapeDtypeStruct((b, n_taps - 1, w), F32)],
        scratch_shapes=[pltpu.VMEM((SUBLANES + tt, w), F32), pltpu.VMEM((tt, w), F32),
                        pltpu.VMEM((tt, w), F32), pltpu.VMEM((1, w), F32)],
        compiler_params=_cparams(("arbitrary", "arbitrary")),
        name="lru_branch",
    )(proj, proj, hist, h0.reshape(b, 1, w), cw, row(cb), wr, row(br), wi, row(bi), row(lam))


def _conf_kernel(dv_ref, dg_ref, hist_ref, cw_ref, cb_ref, g_ref, b_ref, o_ref, nh_ref, gbuf, dcbuf,
                 *, tt, n_taps, pad):
    ti = pl.program_id(1)
    nh = n_taps - 1

    @pl.when(ti == 0)
    def _():
        gbuf[pad - nh:pad, :] = hist_ref[0]

    gbuf[pad:pad + tt, :] = dv_ref[0] * _sigmoid(dg_ref[0])

    def lane_chunk(c, carry):
        c0 = pl.multiple_of(c * LANES, LANES)
        lanes = pl.ds(c0, LANES)
        acc = cb_ref[:, lanes] + cw_ref[0:1, lanes] * gbuf[pad - nh:pad - nh + tt, lanes]
        for k in range(1, n_taps):
            acc = acc + cw_ref[k:k + 1, lanes] * gbuf[pad - nh + k:pad - nh + k + tt, lanes]
        dcbuf[:, lanes] = acc
        return carry

    lax.fori_loop(0, gbuf.shape[1] // LANES, lane_chunk, 0)
    dc = dcbuf[...]
    mu = jnp.mean(dc, axis=-1, keepdims=True)
    yc = dc - mu
    var = jnp.mean(yc * yc, axis=-1, keepdims=True)
    y = yc * lax.rsqrt(var + LN_EPS) * g_ref[...] + b_ref[...]
    o_ref[0] = (y * _sigmoid(y)).astype(o_ref.dtype)
    last = gbuf[pad + tt - nh:pad + tt, :]
    nh_ref[0] = last
    gbuf[pad - nh:pad, :] = last


def conf_branch(proj, hist, cw, cb, g, bta):
    b, l, _ = proj.shape
    w = hist.shape[-1]
    n_taps = cw.shape[0]
    tt = _tile(l, 256)
    assert tt >= n_taps - 1 and w % LANES == 0
    pad = -(-(n_taps - 1) // SUBLANES) * SUBLANES
    par = pl.BlockSpec((1, w), lambda i, j: (0, 0))
    row = lambda a: a.reshape(1, w)
    return pl.pallas_call(
        functools.partial(_conf_kernel, tt=tt, n_taps=n_taps, pad=pad),
        grid=(b, l // tt),
        in_specs=[pl.BlockSpec((1, tt, w), lambda i, j: (i, j, 2)),
                  pl.BlockSpec((1, tt, w), lambda i, j: (i, j, 3)),
                  pl.BlockSpec((1, n_taps - 1, w), lambda i, j: (i, 0, 0)),
                  pl.BlockSpec((n_taps, w), lambda i, j: (0, 0)),
                  par, par, par],
        out_specs=[pl.BlockSpec((1, tt, w), lambda i, j: (i, j, 0)),
                   pl.BlockSpec((1, n_taps - 1, w), lambda i, j: (i, 0, 0))],
        out_shape=[jax.ShapeDtypeStruct((b, l, w), BF16),
                   jax.ShapeDtypeStruct((b, n_taps - 1, w), F32)],
        scratch_shapes=[pltpu.VMEM((pad + tt, w), F32), pltpu.VMEM((tt, w), F32)],
        compiler_params=_cparams(("arbitrary", "arbitrary")),
        name="conf_branch",
    )(proj, proj, hist, cw, row(cb), row(g), row(bta))


def _extract_topk(vals, payload, k):
    n = vals.shape[0]
    iota = lax.broadcasted_iota(jnp.int32, vals.shape, 0).astype(F32)
    top_v, top_p = [], []
    for _ in range(k):
        m = jnp.max(vals, axis=0, keepdims=True)
        idx = jnp.min(jnp.where(vals == m, iota, float(n)), axis=0, keepdims=True)
        sel = iota == idx
        top_v.append(m)
        if payload is None:
            top_p.append(idx)
        else:
            top_p.append(jnp.max(jnp.where(sel, payload, -1.0), axis=0, keepdims=True))
        vals = jnp.where(sel, -jnp.inf, vals)
    return jnp.concatenate(top_v, axis=0), jnp.concatenate(top_p, axis=0)


def _peer_route_kernel(q_ref, keys_ref, e_ref, g_ref, *, n_heads, n_keys, half):
    topk = PEER_TOPK
    for h in range(n_heads):
        tops = []
        for p in range(2):
            c0 = (h * 2 + p) * half
            s = lax.dot_general(keys_ref[p], q_ref[:, c0:c0 + half], (((1,), (1,)), ((), ())),
                                precision=HIGHEST, preferred_element_type=F32)
            tops.append(_extract_topk(s, None, topk))
        (s1, i1), (s2, i2) = tops
        cand = jnp.concatenate([s1[i:i + 1] + s2 for i in range(topk)], axis=0)
        cand_e = jnp.concatenate([i1[i:i + 1] * float(n_keys) + i2 for i in range(topk)], axis=0)
        top, e = _extract_topk(cand, cand_e, topk)
        ex = jnp.exp(top - top[0:1])
        gate = ex / jnp.sum(ex, axis=0, keepdims=True)
        e_ref[h * topk:(h + 1) * topk, :] = e.astype(jnp.int32)
        g_ref[h * topk:(h + 1) * topk, :] = gate


def peer_route(q2d, keys):
    t, qw = q2d.shape
    _, n_keys, half = keys.shape
    n_heads = qw // (2 * half)
    tt = _tile(t, 256, LANES)
    rows = n_heads * PEER_TOPK
    return pl.pallas_call(
        functools.partial(_peer_route_kernel, n_heads=n_heads, n_keys=n_keys, half=half),
        grid=(t // tt,),
        in_specs=[pl.BlockSpec((tt, qw), lambda i: (i, 0)),
                  pl.BlockSpec(keys.shape, lambda i: (0, 0, 0))],
        out_specs=[pl.BlockSpec((rows, tt), lambda i: (0, i))] * 2,
        out_shape=[jax.ShapeDtypeStruct((rows, t), jnp.int32), jax.ShapeDtypeStruct((rows, t), F32)],
        compiler_params=_cparams(("arbitrary",)),
        name="peer_route",
    )(q2d, keys)


def _peer_gates_kernel(e_ref, g_ref, w_ref, *, n_keys, tt):
    shift = int(math.log2(n_keys))
    sub = lax.broadcasted_iota(jnp.int32, (n_keys, e_ref.shape[1]), 0)

    def body(t, carry):
        e = e_ref[pl.ds(t, 1), :]
        gt = g_ref[pl.ds(t, 1), :]
        a_hot = jnp.where((e >> shift) == sub, 1.0, 0.0).astype(BF16)
        b_hot = jnp.where((e & (n_keys - 1)) == sub, gt, 0.0).astype(BF16)
        w = lax.dot_general(a_hot, b_hot, (((1,), (1,)), ((), ())), preferred_element_type=F32)
        w_ref[t] = w.astype(w_ref.dtype)
        return carry

    lax.fori_loop(0, tt, body, 0)


def peer_gates(e_tm, g_tm, n_keys):
    t, ns = e_tm.shape
    assert n_keys & (n_keys - 1) == 0
    tt = _tile(t, 128)
    return pl.pallas_call(
        functools.partial(_peer_gates_kernel, n_keys=n_keys, tt=tt),
        grid=(t // tt,),
        in_specs=[pl.BlockSpec((tt, ns), lambda i: (i, 0))] * 2,
        out_specs=pl.BlockSpec((tt, n_keys, n_keys), lambda i: (i, 0, 0)),
        out_shape=jax.ShapeDtypeStruct((t, n_keys, n_keys), BF16),
        compiler_params=_cparams(("arbitrary",)),
        name="peer_gates",
    )(e_tm, g_tm)


def _peer_dense_kernel(x_ref, ut_ref, v_ref, w_ref, o_ref):
    c = pl.program_id(1)
    hid = jnp.dot(x_ref[...], ut_ref[...], preferred_element_type=F32)
    act = (w_ref[...].astype(F32) * _gelu(hid)).astype(BF16)
    part = jnp.dot(act, v_ref[...], preferred_element_type=F32)

    @pl.when(c == 0)
    def _():
        o_ref[...] = part

    @pl.when(c != 0)
    def _():
        o_ref[...] += part


def peer_dense(x2d, u_t, v, w2d):
    t, d = x2d.shape
    e = v.shape[0]
    tt = _tile(t, 512)
    ec = _tile(e, 512, LANES)
    return pl.pallas_call(
        _peer_dense_kernel, grid=(t // tt, e // ec),
        in_specs=[pl.BlockSpec((tt, d), lambda i, c: (i, 0)),
                  pl.BlockSpec((d, ec), lambda i, c: (0, c)),
                  pl.BlockSpec((ec, d), lambda i, c: (c, 0)),
                  pl.BlockSpec((tt, ec), lambda i, c: (i, c))],
        out_specs=pl.BlockSpec((tt, d), lambda i, c: (i, 0)),
        out_shape=jax.ShapeDtypeStruct((t, d), F32),
        compiler_params=_cparams(("arbitrary", "arbitrary")),
        name="peer_dense",
    )(x2d, u_t, v, w2d)


def peer(h, wq, keys, u_t, v):
    b, l, d = h.shape
    t = b * l
    h2d = h.reshape(t, d)
    q = matmul([h2d], wq)
    e_t, g_t = peer_route(q, keys)
    n_keys = keys.shape[1]
    w = peer_gates(e_t.T, g_t.T, n_keys).reshape(t, n_keys * n_keys)
    return peer_dense(h2d, u_t, v, w).reshape(b, l, d)


def _even_mixer(h, k_past, v_past, s_re, s_im, p, e):
    b, l, d = h.shape
    n_heads, dh = p['n_heads'], p['dh']
    sbw = n_heads * dh
    proj = matmul([h.reshape(b * l, d)], p['w_in_e'][e])
    ncols = proj.shape[1]
    proj3 = proj.reshape(b, l, ncols)
    att = sb_attention(proj3, n_heads, dh, k_past, v_past)
    u = proj3[..., 3 * sbw:]
    ssw = u.shape[-1]
    y, s_re, s_im = s5_ssm(u, s_re, s_im, p['s5_mats'][e])
    assert (3 * sbw) % ssw == 0
    ssm_out = ssm_glu(y.reshape(b * l, ssw), proj, 3 * sbw // ssw, p['ssm_d'][e], p['ssm_w_glu'][e])
    out = matmul([att.reshape(b * l, sbw), ssm_out], p['w_out_e'][e]).reshape(b, l, d)
    k = proj3[..., sbw:2 * sbw].reshape(b, l, n_heads, dh)
    v = proj3[..., 2 * sbw:3 * sbw].reshape(b, l, n_heads, dh)
    return out, k, v, s_re, s_im


def _odd_mixer(h, lru_h, lru_hist, d_hist, p, o):
    b, l, d = h.shape
    proj = matmul([h.reshape(b * l, d)], p['w_in_o'][o])
    proj3 = proj.reshape(b, l, proj.shape[1])
    lru_out, h_last, new_lru_hist = lru_branch(
        proj3, lru_hist, lru_h, p['lru_conv_w'][o], p['lru_conv_b'][o], p['lru_w_r'][o], p['lru_b_r'][o],
        p['lru_w_i'][o], p['lru_b_i'][o], p['lru_lam'][o])
    conf_out, new_d_hist = conf_branch(proj3, d_hist, p['dconv_w'][o], p['dconv_b'][o],
                                       p['dnorm_g'][o], p['dnorm_b'][o])
    w = lru_out.shape[-1]
    out = matmul([lru_out.reshape(b * l, w), conf_out.reshape(b * l, conf_out.shape[-1])],
                 p['w_out_o'][o]).reshape(b, l, d)
    return out, h_last.reshape(b, w), new_lru_hist, new_d_hist


def _trunk(x, mod, k_past, v_past, ssm_re, ssm_im, lru_h, lru_hist, d_hist, p):
    depth = p['depth']
    alpha = (2.0 * depth) ** 0.25
    d = x.shape[-1]

    def mod_of(l, s):
        m = mod[l * 2 + s][:, None, :]
        return m[..., :d], m[..., d:2 * d], m[..., 2 * d:]

    ks, vs, sres, sims, lhs, lcs, dcs = [], [], [], [], [], [], []
    shift, scale, gate = mod_of(0, 0)
    h = modulate(x, scale, shift)
    for l in range(depth):
        if l % 2 == 0:
            e = l // 2
            kp = None if k_past is None else k_past[e]
            vp = None if v_past is None else v_past[e]
            out, k, v, s_re, s_im = _even_mixer(h, kp, vp, ssm_re[e], ssm_im[e], p, e)
            ks.append(k)
            vs.append(v)
            sres.append(s_re)
            sims.append(s_im)
        else:
            o = l // 2
            out, hl, lc, dc = _odd_mixer(h, lru_h[o], lru_hist[o], d_hist[o], p, o)
            lhs.append(hl)
            lcs.append(lc)
            dcs.append(dc)
        shift2, scale2, gate2 = mod_of(l, 1)
        x, h = res_ln(x, out, gate, p['ln_g'][l, 0], p['ln_b'][l, 0], alpha, nxt=(scale2, shift2))
        out = peer(h, p['peer_wq'][l], p['peer_keys'][l], p['peer_ut'][l], p['peer_v'][l])
        if l + 1 < depth:
            shift, scale, gate_n = mod_of(l + 1, 0)
            x, h = res_ln(x, out, gate2, p['ln_g'][l, 1], p['ln_b'][l, 1], alpha, nxt=(scale, shift))
            gate = gate_n
        else:
            x, _ = res_ln(x, out, gate2, p['ln_g'][l, 1], p['ln_b'][l, 1], alpha)
    return (x, jnp.stack(ks), jnp.stack(vs), jnp.stack(sres), jnp.stack(sims),
            jnp.stack(lhs), jnp.stack(lcs), jnp.stack(dcs))


def kernel(x_prompt, x_sample, c_prompt, c_sample, cache_k, cache_v, state_ssm_re, state_ssm_im, state_lru, state_lru_conv, state_dconv, ada_w, ada_b, ln_g, ln_b, w_in_e, w_out_e, ssm_a_re, ssm_a_im, ssm_log_dt, ssm_b_re, ssm_b_im, ssm_c_re, ssm_c_im, ssm_d, ssm_w_glu, w_in_o, w_out_o, lru_conv_w, lru_conv_b, lru_w_r, lru_b_r, lru_w_i, lru_b_i, lru_lam, dconv_w, dconv_b, dnorm_g, dnorm_b, peer_wq, peer_keys, peer_u, peer_v):
    depth = ada_w.shape[0]
    d = x_prompt.shape[-1]
    n_even, n_odd = w_in_e.shape[0], w_in_o.shape[0]
    bp, bs = x_prompt.shape[0], x_sample.shape[0]
    n_heads, dh = cache_k.shape[3], cache_k.shape[4]
    g, pst = state_ssm_re.shape[2], state_ssm_re.shape[3]
    lw = state_lru.shape[-1]
    cwid = state_dconv.shape[-1]

    p = dict(
        depth=depth, n_heads=n_heads, dh=dh, ln_g=ln_g, ln_b=ln_b,
        w_in_e=w_in_e.astype(BF16), w_out_e=w_out_e.astype(BF16),
        w_in_o=w_in_o.astype(BF16), w_out_o=w_out_o.astype(BF16),
        ssm_d=ssm_d, ssm_w_glu=ssm_w_glu.astype(BF16),
        s5_mats=[s5_matrices(ssm_a_re[e], ssm_a_im[e], ssm_log_dt[e], ssm_b_re[e], ssm_b_im[e],
                             ssm_c_re[e], ssm_c_im[e]) for e in range(n_even)],
        lru_conv_w=lru_conv_w, lru_conv_b=lru_conv_b, lru_w_r=lru_w_r.astype(BF16), lru_b_r=lru_b_r,
        lru_w_i=lru_w_i.astype(BF16), lru_b_i=lru_b_i, lru_lam=lru_lam,
        dconv_w=dconv_w, dconv_b=dconv_b, dnorm_g=dnorm_g, dnorm_b=dnorm_b,
        peer_wq=peer_wq.astype(BF16), peer_keys=peer_keys,
        peer_ut=jnp.swapaxes(peer_u.astype(BF16), 1, 2), peer_v=peer_v.astype(BF16),
    )

    c_all = jnp.concatenate([c_prompt, c_sample], axis=0)
    mod = ada_modulation(c_all, ada_w.reshape(depth * 2, d, 3 * d), ada_b.reshape(depth * 2, 1, 3 * d))
    mod_p = [mod[i, :bp] for i in range(depth * 2)]
    mod_s = [mod[i, bp:] for i in range(depth * 2)]

    f32 = F32
    ssm0 = jnp.zeros((n_even, bp, g, pst), f32)
    lru0 = jnp.zeros((n_odd, bp, lw), f32)
    lconv0 = jnp.zeros((n_odd, bp) + state_lru_conv.shape[2:], f32)
    dconv0 = jnp.zeros((n_odd, bp) + state_dconv.shape[2:], f32)
    (y_p, k_p, v_p, sre_p, sim_p, lru_p, lc_p, dc_p) = _trunk(
        x_prompt, mod_p, None, None, ssm0, ssm0, lru0, lconv0, dconv0, p)
    past = cache_k.shape[2]
    ck = cache_k.reshape(n_even, bs, past, n_heads * dh)
    cv = cache_v.reshape(n_even, bs, past, n_heads * dh)
    (y_s, k_s, v_s, sre_s, sim_s, lru_s, lc_s, dc_s) = _trunk(
        x_sample, mod_s, ck, cv, state_ssm_re, state_ssm_im, state_lru, state_lru_conv, state_dconv, p)
    return (y_p, y_s, k_p, v_p, k_s, v_s, sre_p, sim_p, sre_s, sim_s,
            lru_p, lru_s, lc_p, lc_s, dc_p, dc_s)
```

```python
import functools
import math

import jax
import jax.numpy as jnp
from jax import lax
from jax.experimental import pallas as pl
from jax.experimental.pallas import tpu as pltpu

F32 = jnp.float32
BF16 = jnp.bfloat16
HIGHEST = lax.Precision.HIGHEST

LANES = 128
SUBLANES = 8
VMEM_LIMIT_BYTES = 56 * 1024 * 1024

LN_EPS = 1e-5
LRU_C = 8.0
PEER_TOPK = 16
SSM_CHUNK = 16


def _cparams(sem):
    return pltpu.CompilerParams(dimension_semantics=sem, vmem_limit_bytes=VMEM_LIMIT_BYTES)


def _tile(n, pref, mult=SUBLANES):
    if n <= pref:
        return n
    t = (pref // mult) * mult
    while n % t:
        t -= mult
    return t


def _gelu(x):
    return jax.nn.gelu(x, approximate=True)


def _sigmoid(x):
    return jax.nn.sigmoid(x)


def _ada_kernel(c_ref, w_ref, b_ref, o_ref):
    c = c_ref[...]
    s = (c * _sigmoid(c)).astype(BF16)
    o_ref[0] = jnp.dot(s, w_ref[0].astype(BF16), preferred_element_type=F32) + b_ref[0]


def ada_modulation(c_all, ada_w, ada_b):
    nl, d, n3 = ada_w.shape
    r = c_all.shape[0]
    tn = _tile(n3, 768, LANES)
    return pl.pallas_call(
        _ada_kernel,
        grid=(nl, n3 // tn),
        in_specs=[pl.BlockSpec((r, d), lambda l, j: (0, 0)),
                  pl.BlockSpec((1, d, tn), lambda l, j: (l, 0, j)),
                  pl.BlockSpec((1, 1, tn), lambda l, j: (l, 0, j))],
        out_specs=pl.BlockSpec((1, r, tn), lambda l, j: (l, 0, j)),
        out_shape=jax.ShapeDtypeStruct((nl, r, n3), F32),
        compiler_params=_cparams(("arbitrary", "arbitrary")),
        name="ada_modulation",
    )(c_all, ada_w, ada_b)


def _modulate_kernel(x_ref, sc_ref, sh_ref, o_ref):
    o_ref[0] = (x_ref[0] * (1.0 + sc_ref[0]) + sh_ref[0]).astype(o_ref.dtype)


def modulate(x, scale, shift):
    b, l, d = x.shape
    tl = _tile(l, 512)
    vec = pl.BlockSpec((1, 1, d), lambda i, j: (i, 0, 0))
    blk = pl.BlockSpec((1, tl, d), lambda i, j: (i, j, 0))
    return pl.pallas_call(
        _modulate_kernel, grid=(b, l // tl),
        in_specs=[blk, vec, vec], out_specs=blk,
        out_shape=jax.ShapeDtypeStruct((b, l, d), BF16),
        compiler_params=_cparams(("arbitrary", "arbitrary")),
        name="modulate",
    )(x, scale, shift)


def _res_ln_kernel(*refs, alpha, with_next):
    if with_next:
        x_ref, o_ref, gate_ref, g_ref, b_ref, sc_ref, sh_ref, xn_ref, hn_ref = refs
    else:
        x_ref, o_ref, gate_ref, g_ref, b_ref, xn_ref = refs
    y = alpha * x_ref[0] + (1.0 + gate_ref[0]) * o_ref[0]
    mu = jnp.mean(y, axis=-1, keepdims=True)
    yc = y - mu
    var = jnp.mean(yc * yc, axis=-1, keepdims=True)
    xn = yc * lax.rsqrt(var + LN_EPS) * g_ref[...] + b_ref[...]
    xn_ref[0] = xn
    if with_next:
        hn_ref[0] = (xn * (1.0 + sc_ref[0]) + sh_ref[0]).astype(hn_ref.dtype)


def res_ln(x, out, gate, g, b, alpha, nxt=None):
    bsz, l, d = x.shape
    tl = _tile(l, 256)
    vec = pl.BlockSpec((1, 1, d), lambda i, j: (i, 0, 0))
    par = pl.BlockSpec((1, d), lambda i, j: (0, 0))
    blk = pl.BlockSpec((1, tl, d), lambda i, j: (i, j, 0))
    ins = [x, out, gate, g.reshape(1, d), b.reshape(1, d)]
    in_specs = [blk, blk, vec, par, par]
    out_shape = [jax.ShapeDtypeStruct((bsz, l, d), F32)]
    out_specs = [blk]
    if nxt is not None:
        ins += [nxt[0], nxt[1]]
        in_specs += [vec, vec]
        out_shape.append(jax.ShapeDtypeStruct((bsz, l, d), BF16))
        out_specs.append(blk)
    res = pl.pallas_call(
        functools.partial(_res_ln_kernel, alpha=alpha, with_next=nxt is not None),
        grid=(bsz, l // tl), in_specs=in_specs, out_specs=out_specs, out_shape=out_shape,
        compiler_params=_cparams(("arbitrary", "arbitrary")),
        name="res_ln",
    )(*ins)
    return (res[0], res[1]) if nxt is not None else (res[0], None)


def _mm_kernel(*refs, k_sizes):
    n_x = len(k_sizes)
    w_ref, o_ref = refs[n_x], refs[n_x + 1]
    acc = None
    off = 0
    for x_ref, ks in zip(refs[:n_x], k_sizes):
        part = jnp.dot(x_ref[...], w_ref[0, off:off + ks, :], preferred_element_type=F32)
        acc = part if acc is None else acc + part
        off += ks
    o_ref[...] = acc.astype(o_ref.dtype)


def matmul(xs, w_all, layer, out_dtype=F32, tm_pref=1024, tn_pref=512):
    m = xs[0].shape[0]
    _, k, n = w_all.shape
    k_sizes = tuple(x.shape[1] for x in xs)
    assert sum(k_sizes) == k
    tm = _tile(m, tm_pref)
    tn = _tile(n, tn_pref, LANES)
    in_specs = [pl.BlockSpec((tm, ks), lambda i, j: (i, 0)) for ks in k_sizes]
    in_specs.append(pl.BlockSpec((1, k, tn), lambda i, j: (layer, 0, j)))
    return pl.pallas_call(
        functools.partial(_mm_kernel, k_sizes=k_sizes),
        grid=(m // tm, n // tn), in_specs=in_specs,
        out_specs=pl.BlockSpec((tm, tn), lambda i, j: (i, j)),
        out_shape=jax.ShapeDtypeStruct((m, n), out_dtype),
        compiler_params=_cparams(("arbitrary", "arbitrary")),
        name="matmul",
    )(*xs, w_all)


def _suffix_sum_matrix(kb):
    tri = (lax.broadcasted_iota(jnp.int32, (kb, kb), 0)
           > lax.broadcasted_iota(jnp.int32, (kb, kb), 1)).astype(BF16)
    one = jnp.concatenate([tri, jnp.ones((kb, LANES), BF16)], axis=1)
    return jnp.concatenate([one, one], axis=0)


def _sb_attn_kernel(*refs, qb, kb_past, n_past, scale, hp, dh):
    if n_past:
        q_ref, k_ref, v_ref, pk_ref, pv_ref, sq_ref, sp_ref, o_ref = refs
    else:
        q_ref, k_ref, v_ref, sq_ref, o_ref = refs
    qi = pl.program_id(2)
    qs = (q_ref[0] * scale).astype(BF16)

    def block(h, kblk, vblk, carry, acc, mask, kb, smat):
        z = lax.dot_general(qs[:, h * dh:(h + 1) * dh], kblk.astype(BF16), (((1,), (1,)), ((), ())),
                            preferred_element_type=F32)
        soft = jnp.log(1.0 + jnp.exp(-jnp.abs(z)))
        log_sig = jnp.minimum(z, 0.0) - soft
        log_keep = log_sig - z
        if mask is not None:
            log_keep = jnp.where(mask, log_keep, 0.0)
        hi = log_keep.astype(BF16)
        lo = (log_keep - hi.astype(F32)).astype(BF16)
        sums = jnp.dot(jnp.concatenate([hi, lo], axis=1), smat, preferred_element_type=F32)
        later = sums[:, 0:kb]
        if kb <= LANES:
            carry_b = carry[:, 0:kb]
        else:
            carry_b = jnp.concatenate([carry] * (kb // LANES), axis=1)
        w = jnp.exp(log_sig + later + carry_b)
        if mask is not None:
            w = jnp.where(mask, w, 0.0)
        acc = acc + jnp.dot(w.astype(BF16), vblk.astype(BF16), preferred_element_type=F32)
        carry = carry + sums[:, kb:]
        return carry, acc

    def heads_step(load, s, kb, state, mask, smat):
        out = []
        for h in range(hp):
            kblk, vblk = load(h, s, kb)
            out.append(block(h, kblk, vblk, state[h][0], state[h][1], mask, kb, smat))
        return tuple(out)

    def load_new(h, s, kb):
        cols = slice(h * dh, (h + 1) * dh)
        return k_ref[0, pl.ds(s, kb), cols], v_ref[0, pl.ds(s, kb), cols]

    def load_past(h, s, kb):
        return pk_ref[0, 0, pl.ds(s, kb), h, :], pv_ref[0, 0, pl.ds(s, kb), h, :]

    row = lax.broadcasted_iota(jnp.int32, (qb, qb), 0)
    col = lax.broadcasted_iota(jnp.int32, (qb, qb), 1)
    state = tuple((jnp.zeros((qb, LANES), F32), jnp.zeros((qb, dh), F32)) for _ in range(hp))
    smat_q = sq_ref[...]
    state = heads_step(load_new, pl.multiple_of(qi * qb, qb), qb, state, col < row, smat_q)

    def new_body(it, st):
        s = pl.multiple_of((qi - 1 - it) * qb, qb)
        return heads_step(load_new, s, qb, st, None, smat_q)

    state = lax.fori_loop(0, qi, new_body, state)

    if n_past:
        smat_p = sp_ref[...]

        def past_body(it, st):
            s = pl.multiple_of((n_past - 1 - it) * kb_past, kb_past)
            return heads_step(load_past, s, kb_past, st, None, smat_p)

        state = lax.fori_loop(0, n_past, past_body, state)
    o_ref[0] = jnp.concatenate([st[1] for st in state], axis=1).astype(o_ref.dtype)


def sb_attention(proj, n_heads, dh, past=None):
    b, l, _ = proj.shape
    qb = _tile(l, 256)
    if past is not None and past[0].shape[2] > 0:
        hp = SUBLANES if n_heads % SUBLANES == 0 else n_heads
    else:
        hp = 2 if n_heads % 2 == 0 else 1
    hw = hp * dh
    hg = n_heads // hp
    n_past, kb_past = 0, qb
    in_specs = [pl.BlockSpec((1, qb, hw), lambda bi, hi, qi: (bi, qi, hi)),
                pl.BlockSpec((1, l, hw), lambda bi, hi, qi: (bi, 0, hg + hi)),
                pl.BlockSpec((1, l, hw), lambda bi, hi, qi: (bi, 0, 2 * hg + hi))]
    ins = [proj, proj, proj]
    const = lambda bi, hi, qi: (0, 0)
    if past is not None and past[0].shape[2] > 0:
        k_past, v_past, e = past
        p = k_past.shape[2]
        kb_past = _tile(p, 256)
        n_past = p // kb_past
        in_specs += [pl.BlockSpec((1, 1, p, hp, dh), lambda bi, hi, qi: (e, bi, 0, hi, 0))] * 2
        ins += [k_past, v_past]
    smat_q = _suffix_sum_matrix(qb)
    in_specs.append(pl.BlockSpec(smat_q.shape, const))
    ins.append(smat_q)
    if n_past:
        smat_p = _suffix_sum_matrix(kb_past)
        in_specs.append(pl.BlockSpec(smat_p.shape, const))
        ins.append(smat_p)
    return pl.pallas_call(
        functools.partial(_sb_attn_kernel, qb=qb, kb_past=kb_past, n_past=n_past, scale=dh ** -0.5,
                          hp=hp, dh=dh),
        grid=(b, hg, l // qb), in_specs=in_specs,
        out_specs=pl.BlockSpec((1, qb, hw), lambda bi, hi, qi: (bi, qi, hi)),
        out_shape=jax.ShapeDtypeStruct((b, l, n_heads * dh), BF16),
        compiler_params=_cparams(("arbitrary", "arbitrary", "arbitrary")),
        name="sb_attention",
    )(*ins)


def _chunk_rows(u_ref, mb):
    return jnp.concatenate([u_ref[pl.ds(s, mb, stride=SSM_CHUNK), :] for s in range(SSM_CHUNK)], axis=1)


def _s5_inject_kernel(u_ref, bc_ref, re_ref, im_ref, *, mb):
    inj = jnp.dot(_chunk_rows(u_ref, mb), bc_ref[0], precision=HIGHEST, preferred_element_type=F32)
    half = inj.shape[1] // 2
    re_ref[...] = inj[:, :half]
    im_ref[...] = inj[:, half:]


def _s5_scan_kernel(ire_ref, iim_ref, h0r_ref, h0i_ref, ar_ref, ai_ref,
                    sre_ref, sim_ref, fre_ref, fim_ref):
    n_chunks = ire_ref.shape[1]
    ar = ar_ref[...]
    ai = ai_ref[...]

    def body(k, h):
        hr, hi = h
        row = pl.ds(k, 1)
        sre_ref[0, row, :] = hr
        sim_ref[0, row, :] = hi
        return (ar * hr - ai * hi + ire_ref[0, row, :], ar * hi + ai * hr + iim_ref[0, row, :])

    hr, hi = lax.fori_loop(0, n_chunks, body, (h0r_ref[0], h0i_ref[0]))
    fre_ref[0] = hr
    fim_ref[0] = hi


def _s5_output_kernel(u_ref, m_ref, sre_ref, sim_ref, cc_ref, y_ref, *, mb):
    ucat = _chunk_rows(u_ref, mb).astype(BF16)
    half = cc_ref.shape[1] // 2
    y = (jnp.dot(ucat, m_ref[0], preferred_element_type=F32)
         + jnp.dot(sre_ref[...].astype(BF16), cc_ref[0, :half, :], preferred_element_type=F32)
         + jnp.dot(sim_ref[...].astype(BF16), cc_ref[0, half:, :], preferred_element_type=F32))
    for t in range(SSM_CHUNK):
        y_ref[pl.ds(t, mb, stride=SSM_CHUNK), :] = y[:, t * LANES:(t + 1) * LANES]


def s5_matrices(a_re, a_im, log_dt, b_re, b_im, c_re, c_im):
    g, p = a_re.shape
    n_in = b_re.shape[-1]
    ch = SSM_CHUNK
    gl = LANES // n_in
    lg = g // gl
    dt = jnp.exp(log_dt)[:, None]
    mag = jnp.exp(a_re * dt)
    abr, abi = mag * jnp.cos(a_im * dt), mag * jnp.sin(a_im * dt)
    den = a_re * a_re + a_im * a_im
    nr, ni = abr - 1.0, abi
    z_re, z_im = (nr * a_re + ni * a_im) / den, (ni * a_re - nr * a_im) / den
    bb_re = z_re[..., None] * b_re - z_im[..., None] * b_im
    bb_im = z_re[..., None] * b_im + z_im[..., None] * b_re
    steps = jnp.arange(ch + 1, dtype=F32)[:, None, None]
    pmag = jnp.exp(steps * (a_re * dt)[None])
    pw_re = pmag * jnp.cos(steps * (a_im * dt)[None])
    pw_im = pmag * jnp.sin(steps * (a_im * dt)[None])
    e_re = c_re[None] * pw_re[:, :, None, :] - c_im[None] * pw_im[:, :, None, :]
    e_im = c_re[None] * pw_im[:, :, None, :] + c_im[None] * pw_re[:, :, None, :]
    kern = (jnp.einsum('ngop,gpi->ngoi', e_re[:ch], bb_re, precision=HIGHEST)
            - jnp.einsum('ngop,gpi->ngoi', e_im[:ch], bb_im, precision=HIGHEST))
    eye = jnp.eye(gl, dtype=F32)
    zk = jnp.zeros_like(kern)
    kg = jnp.stack([jnp.concatenate([zk[:s], kern[:ch - s]], axis=0) for s in range(ch)])
    kg = kg.reshape(ch, ch, lg, gl, n_in, n_in).transpose(2, 0, 3, 5, 1, 4)
    m_big = kg[:, :, :, :, :, None, :] * eye[None, None, :, None, None, :, None]
    m_big = m_big.reshape(lg, ch * LANES, ch * LANES).astype(BF16)
    rp_re = jnp.stack([pw_re[ch - 1 - s] for s in range(ch)])
    rp_im = jnp.stack([pw_im[ch - 1 - s] for s in range(ch)])
    bc_re = rp_re[:, :, :, None] * bb_re[None] - rp_im[:, :, :, None] * bb_im[None]
    bc_im = rp_re[:, :, :, None] * bb_im[None] + rp_im[:, :, :, None] * bb_re[None]

    def bc_big(x):
        y = x.reshape(ch, lg, gl, p, n_in).transpose(1, 0, 2, 4, 3)
        y = y[:, :, :, :, None, :] * eye[None, None, :, None, :, None]
        return y.reshape(lg, ch * LANES, gl * p)

    bc = jnp.concatenate([bc_big(bc_re), bc_big(bc_im)], axis=2)

    def cc_big(x):
        y = x.reshape(ch, lg, gl, n_in, p).transpose(1, 2, 4, 0, 3)
        y = y[:, :, :, :, None, :] * eye[None, :, None, None, :, None]
        return y.reshape(lg, gl * p, ch * LANES)

    cc = jnp.concatenate([cc_big(e_re[1:]), cc_big(-e_im[1:])], axis=1).astype(BF16)
    a_pow = (pw_re[ch].reshape(1, g * p), pw_im[ch].reshape(1, g * p))
    return m_big, bc, cc, a_pow


def s5_ssm(proj2d, u_col0, b, l, h0_re, h0_im, mats):
    m_big, bc, cc, (apr, api) = mats
    g, p = h0_re.shape[1], h0_re.shape[2]
    ch = SSM_CHUNK
    lg = m_big.shape[0]
    w = lg * LANES
    assert l % ch == 0 and u_col0 % LANES == 0
    k = l // ch
    m = k * b
    mb = _tile(m, 256)
    sw = g * p
    swl = sw // lg
    ucol = u_col0 // LANES
    u_spec = pl.BlockSpec((mb * ch, LANES), lambda i, j: (j, ucol + i))
    st_spec = pl.BlockSpec((mb, swl), lambda i, j: (j, i))
    inj_re, inj_im = pl.pallas_call(
        functools.partial(_s5_inject_kernel, mb=mb), grid=(lg, m // mb),
        in_specs=[u_spec, pl.BlockSpec((1,) + bc.shape[1:], lambda i, j: (i, 0, 0))],
        out_specs=[st_spec] * 2,
        out_shape=[jax.ShapeDtypeStruct((m, sw), F32)] * 2,
        compiler_params=_cparams(("arbitrary", "arbitrary")),
        name="s5_inject",
    )(proj2d, bc)
    tw = _tile(sw, 1024, LANES)
    seq = pl.BlockSpec((1, k, tw), lambda i, j: (i, 0, j))
    st = pl.BlockSpec((1, 1, tw), lambda i, j: (i, 0, j))
    par = pl.BlockSpec((1, tw), lambda i, j: (0, j))
    s_re, s_im, f_re, f_im = pl.pallas_call(
        _s5_scan_kernel, grid=(b, sw // tw),
        in_specs=[seq, seq, st, st, par, par],
        out_specs=[seq, seq, st, st],
        out_shape=[jax.ShapeDtypeStruct((b, k, sw), F32)] * 2 + [jax.ShapeDtypeStruct((b, 1, sw), F32)] * 2,
        compiler_params=_cparams(("arbitrary", "arbitrary")),
        name="s5_scan",
    )(inj_re.reshape(b, k, sw), inj_im.reshape(b, k, sw),
      h0_re.reshape(b, 1, sw), h0_im.reshape(b, 1, sw), apr, api)
    y = pl.pallas_call(
        functools.partial(_s5_output_kernel, mb=mb), grid=(lg, m // mb),
        in_specs=[u_spec,
                  pl.BlockSpec((1,) + m_big.shape[1:], lambda i, j: (i, 0, 0)),
                  st_spec, st_spec,
                  pl.BlockSpec((1,) + cc.shape[1:], lambda i, j: (i, 0, 0))],
        out_specs=pl.BlockSpec((mb * ch, LANES), lambda i, j: (j, i)),
        out_shape=jax.ShapeDtypeStruct((b * l, w), F32),
        compiler_params=_cparams(("arbitrary", "arbitrary")),
        name="s5_output",
    )(proj2d, m_big, s_re.reshape(m, sw), s_im.reshape(m, sw), cc)
    return y, f_re.reshape(b, g, p), f_im.reshape(b, g, p)


def _ssm_glu_kernel(y_ref, u_ref, d_ref, w_ref, o_ref):
    gact = _gelu(y_ref[...] + d_ref[...] * u_ref[...])
    gate = _sigmoid(jnp.dot(gact.astype(BF16), w_ref[0], preferred_element_type=F32))
    o_ref[...] = (gact * gate).astype(o_ref.dtype)


def ssm_glu(y2d, proj2d, u_col_block, d, w_glu_all, layer):
    m, w = y2d.shape
    tm = _tile(m, 512)
    return pl.pallas_call(
        _ssm_glu_kernel, grid=(m // tm,),
        in_specs=[pl.BlockSpec((tm, w), lambda i: (i, 0)),
                  pl.BlockSpec((tm, w), lambda i: (i, u_col_block)),
                  pl.BlockSpec((1, w), lambda i: (0, 0)),
                  pl.BlockSpec((1, w, w), lambda i: (layer, 0, 0))],
        out_specs=pl.BlockSpec((tm, w), lambda i: (i, 0)),
        out_shape=jax.ShapeDtypeStruct((m, w), BF16),
        compiler_params=_cparams(("arbitrary",)),
        name="ssm_glu",
    )(y2d, proj2d, d.reshape(1, w), w_glu_all)


def _lru_kernel(x_ref, gate_ref, hist_ref, h0_ref, cw_ref, cb_ref, wr_ref, br_ref, wi_ref, bi_ref,
                lam_ref, o_ref, hl_ref, nh_ref, xbuf, a_buf, b_buf, hcar, *, tt, n_taps):
    ti = pl.program_id(1)
    pad = SUBLANES
    nh = n_taps - 1

    @pl.when(ti == 0)
    def _():
        xbuf[pad - nh:pad, :] = hist_ref[0]
        hcar[...] = h0_ref[0]

    xbuf[pad:pad + tt, :] = x_ref[0]
    xc = cb_ref[...] + cw_ref[0:1, :] * xbuf[pad - nh:pad - nh + tt, :]
    for k in range(1, n_taps):
        xc = xc + cw_ref[k:k + 1, :] * xbuf[pad - nh + k:pad - nh + k + tt, :]
    last = xbuf[pad + tt - nh:pad + tt, :]
    nh_ref[0] = last
    xbuf[pad - nh:pad, :] = last

    nb = wr_ref.shape[0]
    bw = wr_ref.shape[1]
    xcb = xc.astype(BF16)
    r_parts, i_parts = [], []
    for blk in range(nb):
        xs = xcb[:, blk * bw:(blk + 1) * bw]
        r_parts.append(jnp.dot(xs, wr_ref[blk], preferred_element_type=F32))
        i_parts.append(jnp.dot(xs, wi_ref[blk], preferred_element_type=F32))
    r = _sigmoid(jnp.concatenate(r_parts, axis=1) + br_ref[...])
    ig = _sigmoid(jnp.concatenate(i_parts, axis=1) + bi_ref[...])
    nlam = -lam_ref[...]
    softplus = jnp.maximum(nlam, 0.0) + jnp.log1p(jnp.exp(-jnp.abs(nlam)))
    log_a = -LRU_C * r * softplus
    a_buf[...] = jnp.exp(log_a)
    th = jnp.tanh(log_a)
    b_buf[...] = jnp.sqrt(-2.0 * th / (1.0 - th)) * ig * xc

    def body(t, h):
        h = a_buf[pl.ds(t, 1), :] * h + b_buf[pl.ds(t, 1), :]
        b_buf[pl.ds(t, 1), :] = h
        return h

    h = lax.fori_loop(0, tt, body, hcar[...])
    hcar[...] = h
    hl_ref[0] = h
    o_ref[0] = (b_buf[...] * _gelu(gate_ref[0])).astype(o_ref.dtype)


def lru_branch(proj, hist, h0, cw, cb, wr, br, wi, bi, lam):
    b, l, _ = proj.shape
    w = h0.shape[-1]
    n_taps = cw.shape[0]
    tt = _tile(l, 256)
    row = lambda a: a.reshape(1, w)
    par = pl.BlockSpec((1, w), lambda i, j: (0, 0))
    wspec = pl.BlockSpec(wr.shape, lambda i, j: (0, 0, 0))
    return pl.pallas_call(
        functools.partial(_lru_kernel, tt=tt, n_taps=n_taps),
        grid=(b, l // tt),
        in_specs=[pl.BlockSpec((1, tt, w), lambda i, j: (i, j, 0)),
                  pl.BlockSpec((1, tt, w), lambda i, j: (i, j, 1)),
                  pl.BlockSpec((1, n_taps - 1, w), lambda i, j: (i, 0, 0)),
                  pl.BlockSpec((1, 1, w), lambda i, j: (i, 0, 0)),
                  pl.BlockSpec((n_taps, w), lambda i, j: (0, 0)),
                  par, wspec, par, wspec, par, par],
        out_specs=[pl.BlockSpec((1, tt, w), lambda i, j: (i, j, 0)),
                   pl.BlockSpec((1, 1, w), lambda i, j: (i, 0, 0)),
                   pl.BlockSpec((1, n_taps - 1, w), lambda i, j: (i, 0, 0))],
        out_shape=[jax.ShapeDtypeStruct((b, l, w), BF16),
                   jax.ShapeDtypeStruct((b, 1, w), F32),
                   jax.ShapeDtypeStruct((b, n_taps - 1, w), F32)],
        scratch_shapes=[pltpu.VMEM((SUBLANES + tt, w), F32), pltpu.VMEM((tt, w), F32),
                        pltpu.VMEM((tt, w), F32), pltpu.VMEM((1, w), F32)],
        compiler_params=_cparams(("arbitrary", "arbitrary")),
        name="lru_branch",
    )(proj, proj, hist, h0.reshape(b, 1, w), cw, row(cb), wr, row(br), wi, row(bi), row(lam))


def _conf_kernel(dv_ref, dg_ref, hist_ref, cw_ref, cb_ref, g_ref, b_ref, o_ref, nh_ref, gbuf, dcbuf,
                 *, tt, n_taps, pad):
    ti = pl.program_id(1)
    nh = n_taps - 1

    @pl.when(ti == 0)
    def _():
        gbuf[pad - nh:pad, :] = hist_ref[0]

    gbuf[pad:pad + tt, :] = dv_ref[0] * _sigmoid(dg_ref[0])

    def lane_chunk(c, carry):
        c0 = pl.multiple_of(c * LANES, LANES)
        lanes = pl.ds(c0, LANES)
        acc = cb_ref[:, lanes] + cw_ref[0:1, lanes] * gbuf[pad - nh:pad - nh + tt, lanes]
        for k in range(1, n_taps):
            acc = acc + cw_ref[k:k + 1, lanes] * gbuf[pad - nh + k:pad - nh + k + tt, lanes]
        dcbuf[:, lanes] = acc
        return carry

    lax.fori_loop(0, gbuf.shape[1] // LANES, lane_chunk, 0)
    dc = dcbuf[...]
    mu = jnp.mean(dc, axis=-1, keepdims=True)
    yc = dc - mu
    var = jnp.mean(yc * yc, axis=-1, keepdims=True)
    y = yc * lax.rsqrt(var + LN_EPS) * g_ref[...] + b_ref[...]
    o_ref[0] = (y * _sigmoid(y)).astype(o_ref.dtype)
    last = gbuf[pad + tt - nh:pad + tt, :]
    nh_ref[0] = last
    gbuf[pad - nh:pad, :] = last


def conf_branch(proj, hist, cw, cb, g, bta):
    b, l, _ = proj.shape
    w = hist.shape[-1]
    n_taps = cw.shape[0]
    tt = _tile(l, 256)
    assert tt >= n_taps - 1 and w % LANES == 0
    pad = -(-(n_taps - 1) // SUBLANES) * SUBLANES
    par = pl.BlockSpec((1, w), lambda i, j: (0, 0))
    row = lambda a: a.reshape(1, w)
    return pl.pallas_call(
        functools.partial(_conf_kernel, tt=tt, n_taps=n_taps, pad=pad),
        grid=(b, l // tt),
        in_specs=[pl.BlockSpec((1, tt, w), lambda i, j: (i, j, 2)),
                  pl.BlockSpec((1, tt, w), lambda i, j: (i, j, 3)),
                  pl.BlockSpec((1, n_taps - 1, w), lambda i, j: (i, 0, 0)),
                  pl.BlockSpec((n_taps, w), lambda i, j: (0, 0)),
                  par, par, par],
        out_specs=[pl.BlockSpec((1, tt, w), lambda i, j: (i, j, 0)),
                   pl.BlockSpec((1, n_taps - 1, w), lambda i, j: (i, 0, 0))],
        out_shape=[jax.ShapeDtypeStruct((b, l, w), BF16),
                   jax.ShapeDtypeStruct((b, n_taps - 1, w), F32)],
        scratch_shapes=[pltpu.VMEM((pad + tt, w), F32), pltpu.VMEM((tt, w), F32)],
        compiler_params=_cparams(("arbitrary", "arbitrary")),
        name="conf_branch",
    )(proj, proj, hist, cw, row(cb), row(g), row(bta))


def _extract_topk(vals, payload, k):
    n = vals.shape[0]
    iota = lax.broadcasted_iota(jnp.int32, vals.shape, 0).astype(F32)
    top_v, top_p = [], []
    for _ in range(k):
        m = jnp.max(vals, axis=0, keepdims=True)
        idx = jnp.min(jnp.where(vals == m, iota, float(n)), axis=0, keepdims=True)
        sel = iota == idx
        top_v.append(m)
        if payload is None:
            top_p.append(idx)
        else:
            top_p.append(jnp.max(jnp.where(sel, payload, -1.0), axis=0, keepdims=True))
        vals = jnp.where(sel, -jnp.inf, vals)
    return jnp.concatenate(top_v, axis=0), jnp.concatenate(top_p, axis=0)


def _product_candidates(s1, i1, s2, i2, n_keys):
    k = s1.shape[0]
    sub = SUBLANES
    assert k == 2 * sub
    jrow = lax.broadcasted_iota(jnp.int32, (sub, s1.shape[1]), 0)
    vals = [s1[0:1] + s2]
    ids = [i1[0:1] * n_keys + i2]
    for i in range(1, sub):
        limit = k // (i + 1)
        v = s1[i:i + 1] + s2[0:sub]
        vals.append(v if limit >= sub else jnp.where(jrow < limit, v, -jnp.inf))
        ids.append(i1[i:i + 1] * n_keys + i2[0:sub])
    vals.append(s1[sub:k] + s2[0:1])
    ids.append(i1[sub:k] * n_keys + i2[0:1])
    return jnp.concatenate(vals, axis=0), jnp.concatenate(ids, axis=0)


def _peer_route_kernel(q_ref, keys_ref, e_ref, g_ref, *, n_heads, n_keys, half):
    topk = PEER_TOPK
    for h in range(n_heads):
        tops = []
        for p in range(2):
            c0 = (h * 2 + p) * half
            s = lax.dot_general(keys_ref[p], q_ref[:, c0:c0 + half], (((1,), (1,)), ((), ())),
                                precision=HIGHEST, preferred_element_type=F32)
            tops.append(_extract_topk(s, None, topk))
        (s1, i1), (s2, i2) = tops
        cand, cand_e = _product_candidates(s1, i1, s2, i2, float(n_keys))
        top, e = _extract_topk(cand, cand_e, topk)
        ex = jnp.exp(top - top[0:1])
        gate = ex / jnp.sum(ex, axis=0, keepdims=True)
        e_ref[h * topk:(h + 1) * topk, :] = e.astype(jnp.int32)
        g_ref[h * topk:(h + 1) * topk, :] = gate


def peer_route(q2d, keys):
    t, qw = q2d.shape
    _, n_keys, half = keys.shape
    n_heads = qw // (2 * half)
    tt = _tile(t, 256, LANES)
    rows = n_heads * PEER_TOPK
    return pl.pallas_call(
        functools.partial(_peer_route_kernel, n_heads=n_heads, n_keys=n_keys, half=half),
        grid=(t // tt,),
        in_specs=[pl.BlockSpec((tt, qw), lambda i: (i, 0)),
                  pl.BlockSpec(keys.shape, lambda i: (0, 0, 0))],
        out_specs=[pl.BlockSpec((rows, tt), lambda i: (0, i))] * 2,
        out_shape=[jax.ShapeDtypeStruct((rows, t), jnp.int32), jax.ShapeDtypeStruct((rows, t), F32)],
        compiler_params=_cparams(("arbitrary",)),
        name="peer_route",
    )(q2d, keys)


def _peer_gates_kernel(e_ref, g_ref, w_ref, scr, *, n_keys, tt, group):
    shift = int(math.log2(n_keys))
    sub = lax.broadcasted_iota(jnp.int32, (n_keys, e_ref.shape[1]), 0)

    def body(i, carry):
        base = pl.multiple_of(i * group, group)
        for j in range(group):
            e = e_ref[pl.ds(base + j, 1), :]
            gt = g_ref[pl.ds(base + j, 1), :]
            a_hot = jnp.where((e >> shift) == sub, 1.0, 0.0).astype(BF16)
            b_hot = jnp.where((e & (n_keys - 1)) == sub, gt, 0.0).astype(BF16)
            w = lax.dot_general(a_hot, b_hot, (((1,), (1,)), ((), ())), preferred_element_type=F32)
            scr[pl.ds(j, n_keys, stride=group), :] = w
        for a in range(n_keys):
            w_ref[pl.ds(base, group), a * n_keys:(a + 1) * n_keys] = (
                scr[a * group:(a + 1) * group, :].astype(w_ref.dtype))
        return carry

    lax.fori_loop(0, tt // group, body, 0)


def peer_gates(e_tm, g_tm, n_keys):
    t, ns = e_tm.shape
    assert n_keys & (n_keys - 1) == 0
    group = 2 * SUBLANES
    tt = _tile(t, 128, group)
    return pl.pallas_call(
        functools.partial(_peer_gates_kernel, n_keys=n_keys, tt=tt, group=group),
        grid=(t // tt,),
        in_specs=[pl.BlockSpec((tt, ns), lambda i: (i, 0))] * 2,
        out_specs=pl.BlockSpec((tt, n_keys * n_keys), lambda i: (i, 0)),
        out_shape=jax.ShapeDtypeStruct((t, n_keys * n_keys), BF16),
        scratch_shapes=[pltpu.VMEM((n_keys * group, n_keys), F32)],
        compiler_params=_cparams(("arbitrary",)),
        name="peer_gates",
    )(e_tm, g_tm)


def _peer_dense_kernel(x_ref, u_ref, v_ref, w_ref, o_ref):
    @pl.when(pl.program_id(1) == 0)
    def _():
        o_ref[...] = jnp.zeros_like(o_ref)

    hid = lax.dot_general(x_ref[...], u_ref[0], (((1,), (1,)), ((), ())), preferred_element_type=F32)
    act = (w_ref[...].astype(F32) * _gelu(hid)).astype(BF16)
    o_ref[...] += jnp.dot(act, v_ref[0], preferred_element_type=F32)


def peer_dense(x2d, u_all, v_all, layer, w2d):
    t, d = x2d.shape
    e = v_all.shape[1]
    tt = _tile(t, 512)
    ec = _tile(e, 512, LANES)
    return pl.pallas_call(
        _peer_dense_kernel, grid=(t // tt, e // ec),
        in_specs=[pl.BlockSpec((tt, d), lambda i, c: (i, 0)),
                  pl.BlockSpec((1, ec, d), lambda i, c: (layer, c, 0)),
                  pl.BlockSpec((1, ec, d), lambda i, c: (layer, c, 0)),
                  pl.BlockSpec((tt, ec), lambda i, c: (i, c))],
        out_specs=pl.BlockSpec((tt, d), lambda i, c: (i, 0)),
        out_shape=jax.ShapeDtypeStruct((t, d), F32),
        compiler_params=_cparams(("arbitrary", "arbitrary")),
        name="peer_dense",
    )(x2d, u_all, v_all, w2d)


def peer(h, p, layer):
    b, l, d = h.shape
    t = b * l
    h2d = h.reshape(t, d)
    q = matmul([h2d], p['peer_wq'], layer)
    keys = p['peer_keys'][layer]
    e_t, g_t = peer_route(q, keys)
    w = peer_gates(e_t.T, g_t.T, keys.shape[1])
    return peer_dense(h2d, p['peer_u'], p['peer_v'], layer, w).reshape(b, l, d)


def _even_mixer(h, past, s_re, s_im, p, e):
    b, l, d = h.shape
    n_heads, dh = p['n_heads'], p['dh']
    sbw = n_heads * dh
    proj = matmul([h.reshape(b * l, d)], p['w_in_e'], e)
    ncols = proj.shape[1]
    proj3 = proj.reshape(b, l, ncols)
    att = sb_attention(proj3, n_heads, dh, past)
    ssw = ncols - 3 * sbw
    y, s_re, s_im = s5_ssm(proj, 3 * sbw, b, l, s_re, s_im, p['s5_mats'][e])
    assert (3 * sbw) % ssw == 0
    ssm_out = ssm_glu(y, proj, 3 * sbw // ssw, p['ssm_d'][e], p['ssm_w_glu'], e)
    out = matmul([att.reshape(b * l, sbw), ssm_out], p['w_out_e'], e).reshape(b, l, d)
    k = proj3[..., sbw:2 * sbw].reshape(b, l, n_heads, dh)
    v = proj3[..., 2 * sbw:3 * sbw].reshape(b, l, n_heads, dh)
    return out, k, v, s_re, s_im


def _odd_mixer(h, lru_h, lru_hist, d_hist, p, o):
    b, l, d = h.shape
    proj = matmul([h.reshape(b * l, d)], p['w_in_o'], o)
    proj3 = proj.reshape(b, l, proj.shape[1])
    lru_out, h_last, new_lru_hist = lru_branch(
        proj3, lru_hist, lru_h, p['lru_conv_w'][o], p['lru_conv_b'][o], p['lru_w_r'][o], p['lru_b_r'][o],
        p['lru_w_i'][o], p['lru_b_i'][o], p['lru_lam'][o])
    conf_out, new_d_hist = conf_branch(proj3, d_hist, p['dconv_w'][o], p['dconv_b'][o],
                                       p['dnorm_g'][o], p['dnorm_b'][o])
    w = lru_out.shape[-1]
    out = matmul([lru_out.reshape(b * l, w), conf_out.reshape(b * l, conf_out.shape[-1])],
                 p['w_out_o'], o).reshape(b, l, d)
    return out, h_last.reshape(b, w), new_lru_hist, new_d_hist


def _trunk(x, mod, k_past, v_past, ssm_re, ssm_im, lru_h, lru_hist, d_hist, p):
    depth = p['depth']
    alpha = (2.0 * depth) ** 0.25
    d = x.shape[-1]

    def mod_of(l, s):
        m = mod[l * 2 + s][:, None, :]
        return m[..., :d], m[..., d:2 * d], m[..., 2 * d:]

    ks, vs, sres, sims, lhs, lcs, dcs = [], [], [], [], [], [], []
    shift, scale, gate = mod_of(0, 0)
    h = modulate(x, scale, shift)
    for l in range(depth):
        if l % 2 == 0:
            e = l // 2
            past = None if k_past is None else (k_past, v_past, e)
            out, k, v, s_re, s_im = _even_mixer(h, past, ssm_re[e], ssm_im[e], p, e)
            ks.append(k)
            vs.append(v)
            sres.append(s_re)
            sims.append(s_im)
        else:
            o = l // 2
            out, hl, lc, dc = _odd_mixer(h, lru_h[o], lru_hist[o], d_hist[o], p, o)
            lhs.append(hl)
            lcs.append(lc)
            dcs.append(dc)
        shift2, scale2, gate2 = mod_of(l, 1)
        x, h = res_ln(x, out, gate, p['ln_g'][l, 0], p['ln_b'][l, 0], alpha, nxt=(scale2, shift2))
        out = peer(h, p, l)
        if l + 1 < depth:
            shift, scale, gate_n = mod_of(l + 1, 0)
            x, h = res_ln(x, out, gate2, p['ln_g'][l, 1], p['ln_b'][l, 1], alpha, nxt=(scale, shift))
            gate = gate_n
        else:
            x, _ = res_ln(x, out, gate2, p['ln_g'][l, 1], p['ln_b'][l, 1], alpha)
    return (x, jnp.stack(ks), jnp.stack(vs), jnp.stack(sres), jnp.stack(sims),
            jnp.stack(lhs), jnp.stack(lcs), jnp.stack(dcs))


def kernel(x_prompt, x_sample, c_prompt, c_sample, cache_k, cache_v, state_ssm_re, state_ssm_im, state_lru, state_lru_conv, state_dconv, ada_w, ada_b, ln_g, ln_b, w_in_e, w_out_e, ssm_a_re, ssm_a_im, ssm_log_dt, ssm_b_re, ssm_b_im, ssm_c_re, ssm_c_im, ssm_d, ssm_w_glu, w_in_o, w_out_o, lru_conv_w, lru_conv_b, lru_w_r, lru_b_r, lru_w_i, lru_b_i, lru_lam, dconv_w, dconv_b, dnorm_g, dnorm_b, peer_wq, peer_keys, peer_u, peer_v):
    depth = ada_w.shape[0]
    d = x_prompt.shape[-1]
    n_even, n_odd = w_in_e.shape[0], w_in_o.shape[0]
    bp = x_prompt.shape[0]
    n_heads, dh = cache_k.shape[3], cache_k.shape[4]
    g, pst = state_ssm_re.shape[2], state_ssm_re.shape[3]
    lw = state_lru.shape[-1]

    p = dict(
        depth=depth, n_heads=n_heads, dh=dh, ln_g=ln_g, ln_b=ln_b,
        w_in_e=w_in_e.astype(BF16), w_out_e=w_out_e.astype(BF16),
        w_in_o=w_in_o.astype(BF16), w_out_o=w_out_o.astype(BF16),
        ssm_d=ssm_d, ssm_w_glu=ssm_w_glu.astype(BF16),
        s5_mats=[s5_matrices(ssm_a_re[e], ssm_a_im[e], ssm_log_dt[e], ssm_b_re[e], ssm_b_im[e],
                             ssm_c_re[e], ssm_c_im[e]) for e in range(n_even)],
        lru_conv_w=lru_conv_w, lru_conv_b=lru_conv_b, lru_w_r=lru_w_r.astype(BF16), lru_b_r=lru_b_r,
        lru_w_i=lru_w_i.astype(BF16), lru_b_i=lru_b_i, lru_lam=lru_lam,
        dconv_w=dconv_w, dconv_b=dconv_b, dnorm_g=dnorm_g, dnorm_b=dnorm_b,
        peer_wq=peer_wq.astype(BF16), peer_keys=peer_keys,
        peer_u=peer_u.astype(BF16), peer_v=peer_v.astype(BF16),
    )

    c_all = jnp.concatenate([c_prompt, c_sample], axis=0)
    mod = ada_modulation(c_all, ada_w.reshape(depth * 2, d, 3 * d), ada_b.reshape(depth * 2, 1, 3 * d))
    mod_p = [mod[i, :bp] for i in range(depth * 2)]
    mod_s = [mod[i, bp:] for i in range(depth * 2)]

    ssm0 = jnp.zeros((n_even, bp, g, pst), F32)
    lru0 = jnp.zeros((n_odd, bp, lw), F32)
    lconv0 = jnp.zeros((n_odd, bp) + state_lru_conv.shape[2:], F32)
    dconv0 = jnp.zeros((n_odd, bp) + state_dconv.shape[2:], F32)
    (y_p, k_p, v_p, sre_p, sim_p, lru_p, lc_p, dc_p) = _trunk(
        x_prompt, mod_p, None, None, ssm0, ssm0, lru0, lconv0, dconv0, p)
    (y_s, k_s, v_s, sre_s, sim_s, lru_s, lc_s, dc_s) = _trunk(
        x_sample, mod_s, cache_k, cache_v, state_ssm_re, state_ssm_im, state_lru, state_lru_conv, state_dconv, p)
    return (y_p, y_s, k_p, v_p, k_s, v_s, sre_p, sim_p, sre_s, sim_s,
            lru_p, lru_s, lc_p, lc_s, dc_p, dc_s)
```

```python
import functools
import math

import jax
import jax.numpy as jnp
from jax import lax
from jax.experimental import pallas as pl
from jax.experimental.pallas import tpu as pltpu

F32 = jnp.float32
BF16 = jnp.bfloat16
HIGHEST = lax.Precision.HIGHEST

LANES = 128
SUBLANES = 8
VMEM_LIMIT_BYTES = 56 * 1024 * 1024

LN_EPS = 1e-5
LRU_C = 8.0
PEER_TOPK = 16
SSM_CHUNK = 16


def _cparams(sem):
    return pltpu.CompilerParams(dimension_semantics=sem, vmem_limit_bytes=VMEM_LIMIT_BYTES)


def _tile(n, pref, mult=SUBLANES):
    if n <= pref:
        return n
    t = (pref // mult) * mult
    while n % t:
        t -= mult
    return t


def _gelu(x):
    return jax.nn.gelu(x, approximate=True)


def _sigmoid(x):
    return jax.nn.sigmoid(x)


def _ada_kernel(c_ref, w_ref, b_ref, o_ref):
    c = c_ref[...]
    s = (c * _sigmoid(c)).astype(BF16)
    o_ref[0] = jnp.dot(s, w_ref[0].astype(BF16), preferred_element_type=F32) + b_ref[0]


def ada_modulation(c_all, ada_w, ada_b):
    nl, d, n3 = ada_w.shape
    r = c_all.shape[0]
    tn = _tile(n3, 768, LANES)
    return pl.pallas_call(
        _ada_kernel,
        grid=(nl, n3 // tn),
        in_specs=[pl.BlockSpec((r, d), lambda l, j: (0, 0)),
                  pl.BlockSpec((1, d, tn), lambda l, j: (l, 0, j)),
                  pl.BlockSpec((1, 1, tn), lambda l, j: (l, 0, j))],
        out_specs=pl.BlockSpec((1, r, tn), lambda l, j: (l, 0, j)),
        out_shape=jax.ShapeDtypeStruct((nl, r, n3), F32),
        compiler_params=_cparams(("arbitrary", "arbitrary")),
        name="ada_modulation",
    )(c_all, ada_w, ada_b)


def _modulate_kernel(x_ref, sc_ref, sh_ref, o_ref):
    o_ref[0] = (x_ref[0] * (1.0 + sc_ref[0]) + sh_ref[0]).astype(o_ref.dtype)


def modulate(x, scale, shift):
    b, l, d = x.shape
    tl = _tile(l, 512)
    vec = pl.BlockSpec((1, 1, d), lambda i, j: (i, 0, 0))
    blk = pl.BlockSpec((1, tl, d), lambda i, j: (i, j, 0))
    return pl.pallas_call(
        _modulate_kernel, grid=(b, l // tl),
        in_specs=[blk, vec, vec], out_specs=blk,
        out_shape=jax.ShapeDtypeStruct((b, l, d), BF16),
        compiler_params=_cparams(("arbitrary", "arbitrary")),
        name="modulate",
    )(x, scale, shift)


def _res_ln_kernel(*refs, alpha, with_next):
    if with_next:
        x_ref, o_ref, gate_ref, g_ref, b_ref, sc_ref, sh_ref, xn_ref, hn_ref = refs
    else:
        x_ref, o_ref, gate_ref, g_ref, b_ref, xn_ref = refs
    y = alpha * x_ref[0] + (1.0 + gate_ref[0]) * o_ref[0]
    mu = jnp.mean(y, axis=-1, keepdims=True)
    yc = y - mu
    var = jnp.mean(yc * yc, axis=-1, keepdims=True)
    xn = yc * lax.rsqrt(var + LN_EPS) * g_ref[...] + b_ref[...]
    xn_ref[0] = xn
    if with_next:
        hn_ref[0] = (xn * (1.0 + sc_ref[0]) + sh_ref[0]).astype(hn_ref.dtype)


def res_ln(x, out, gate, g, b, alpha, nxt=None):
    bsz, l, d = x.shape
    tl = _tile(l, 256)
    vec = pl.BlockSpec((1, 1, d), lambda i, j: (i, 0, 0))
    par = pl.BlockSpec((1, d), lambda i, j: (0, 0))
    blk = pl.BlockSpec((1, tl, d), lambda i, j: (i, j, 0))
    ins = [x, out, gate, g.reshape(1, d), b.reshape(1, d)]
    in_specs = [blk, blk, vec, par, par]
    out_shape = [jax.ShapeDtypeStruct((bsz, l, d), F32)]
    out_specs = [blk]
    if nxt is not None:
        ins += [nxt[0], nxt[1]]
        in_specs += [vec, vec]
        out_shape.append(jax.ShapeDtypeStruct((bsz, l, d), BF16))
        out_specs.append(blk)
    res = pl.pallas_call(
        functools.partial(_res_ln_kernel, alpha=alpha, with_next=nxt is not None),
        grid=(bsz, l // tl), in_specs=in_specs, out_specs=out_specs, out_shape=out_shape,
        compiler_params=_cparams(("arbitrary", "arbitrary")),
        name="res_ln",
    )(*ins)
    return (res[0], res[1]) if nxt is not None else (res[0], None)


def _mm_kernel(*refs, k_sizes):
    n_x = len(k_sizes)
    w_ref, o_ref = refs[n_x], refs[n_x + 1]
    acc = None
    off = 0
    for x_ref, ks in zip(refs[:n_x], k_sizes):
        part = jnp.dot(x_ref[...], w_ref[0, off:off + ks, :], preferred_element_type=F32)
        acc = part if acc is None else acc + part
        off += ks
    o_ref[...] = acc.astype(o_ref.dtype)


def matmul(xs, w_all, layer, out_dtype=F32, tm_pref=1024, tn_pref=512):
    m = xs[0].shape[0]
    _, k, n = w_all.shape
    k_sizes = tuple(x.shape[1] for x in xs)
    assert sum(k_sizes) == k
    tm = _tile(m, tm_pref)
    tn = _tile(n, tn_pref, LANES)
    in_specs = [pl.BlockSpec((tm, ks), lambda i, j: (i, 0)) for ks in k_sizes]
    in_specs.append(pl.BlockSpec((1, k, tn), lambda i, j: (layer, 0, j)))
    return pl.pallas_call(
        functools.partial(_mm_kernel, k_sizes=k_sizes),
        grid=(m // tm, n // tn), in_specs=in_specs,
        out_specs=pl.BlockSpec((tm, tn), lambda i, j: (i, j)),
        out_shape=jax.ShapeDtypeStruct((m, n), out_dtype),
        compiler_params=_cparams(("arbitrary", "arbitrary")),
        name="matmul",
    )(*xs, w_all)


def _suffix_sum_matrix(kb):
    tri = (lax.broadcasted_iota(jnp.int32, (kb, kb), 0)
           > lax.broadcasted_iota(jnp.int32, (kb, kb), 1)).astype(BF16)
    one = jnp.concatenate([tri, jnp.ones((kb, LANES), BF16)], axis=1)
    return jnp.concatenate([one, one], axis=0)


def _sb_attn_kernel(*refs, qb, kb_past, n_past, scale, hp, dh):
    if n_past:
        q_ref, k_ref, v_ref, pk_ref, pv_ref, sq_ref, sp_ref, o_ref = refs
    else:
        q_ref, k_ref, v_ref, sq_ref, o_ref = refs
    qi = pl.program_id(2)
    qs = (q_ref[0] * scale).astype(BF16)

    def block(h, kblk, vblk, carry, acc, mask, kb, smat):
        z = lax.dot_general(qs[:, h * dh:(h + 1) * dh], kblk.astype(BF16), (((1,), (1,)), ((), ())),
                            preferred_element_type=F32)
        soft = jnp.log(1.0 + jnp.exp(-jnp.abs(z)))
        log_sig = jnp.minimum(z, 0.0) - soft
        log_keep = log_sig - z
        if mask is not None:
            log_keep = jnp.where(mask, log_keep, 0.0)
        hi = log_keep.astype(BF16)
        lo = (log_keep - hi.astype(F32)).astype(BF16)
        sums = jnp.dot(jnp.concatenate([hi, lo], axis=1), smat, preferred_element_type=F32)
        later = sums[:, 0:kb]
        if kb <= LANES:
            carry_b = carry[:, 0:kb]
        else:
            carry_b = jnp.concatenate([carry] * (kb // LANES), axis=1)
        w = jnp.exp(log_sig + later + carry_b)
        if mask is not None:
            w = jnp.where(mask, w, 0.0)
        acc = acc + jnp.dot(w.astype(BF16), vblk.astype(BF16), preferred_element_type=F32)
        carry = carry + sums[:, kb:]
        return carry, acc

    def heads_step(load, s, kb, state, mask, smat):
        out = []
        for h in range(hp):
            kblk, vblk = load(h, s, kb)
            out.append(block(h, kblk, vblk, state[h][0], state[h][1], mask, kb, smat))
        return tuple(out)

    def load_new(h, s, kb):
        cols = slice(h * dh, (h + 1) * dh)
        return k_ref[0, pl.ds(s, kb), cols], v_ref[0, pl.ds(s, kb), cols]

    def load_past(h, s, kb):
        return pk_ref[0, 0, pl.ds(s, kb), h, :], pv_ref[0, 0, pl.ds(s, kb), h, :]

    row = lax.broadcasted_iota(jnp.int32, (qb, qb), 0)
    col = lax.broadcasted_iota(jnp.int32, (qb, qb), 1)
    state = tuple((jnp.zeros((qb, LANES), F32), jnp.zeros((qb, dh), F32)) for _ in range(hp))
    smat_q = sq_ref[...]
    state = heads_step(load_new, pl.multiple_of(qi * qb, qb), qb, state, col < row, smat_q)

    def new_body(it, st):
        s = pl.multiple_of((qi - 1 - it) * qb, qb)
        return heads_step(load_new, s, qb, st, None, smat_q)

    state = lax.fori_loop(0, qi, new_body, state)

    if n_past:
        smat_p = sp_ref[...]

        def past_body(it, st):
            s = pl.multiple_of((n_past - 1 - it) * kb_past, kb_past)
            return heads_step(load_past, s, kb_past, st, None, smat_p)

        state = lax.fori_loop(0, n_past, past_body, state)
    o_ref[0] = jnp.concatenate([st[1] for st in state], axis=1).astype(o_ref.dtype)


def sb_attention(proj, n_heads, dh, past=None):
    b, l, _ = proj.shape
    qb = _tile(l, 256)
    if past is not None and past[0].shape[2] > 0:
        hp = SUBLANES if n_heads % SUBLANES == 0 else n_heads
    else:
        hp = 4 if n_heads % 4 == 0 else 1
    hw = hp * dh
    hg = n_heads // hp
    n_past, kb_past = 0, qb
    in_specs = [pl.BlockSpec((1, qb, hw), lambda bi, hi, qi: (bi, qi, hi)),
                pl.BlockSpec((1, l, hw), lambda bi, hi, qi: (bi, 0, hg + hi)),
                pl.BlockSpec((1, l, hw), lambda bi, hi, qi: (bi, 0, 2 * hg + hi))]
    ins = [proj, proj, proj]
    const = lambda bi, hi, qi: (0, 0)
    if past is not None and past[0].shape[2] > 0:
        k_past, v_past, e = past
        p = k_past.shape[2]
        kb_past = _tile(p, 256)
        n_past = p // kb_past
        in_specs += [pl.BlockSpec((1, 1, p, hp, dh), lambda bi, hi, qi: (e, bi, 0, hi, 0))] * 2
        ins += [k_past, v_past]
    smat_q = _suffix_sum_matrix(qb)
    in_specs.append(pl.BlockSpec(smat_q.shape, const))
    ins.append(smat_q)
    if n_past:
        smat_p = _suffix_sum_matrix(kb_past)
        in_specs.append(pl.BlockSpec(smat_p.shape, const))
        ins.append(smat_p)
    return pl.pallas_call(
        functools.partial(_sb_attn_kernel, qb=qb, kb_past=kb_past, n_past=n_past, scale=dh ** -0.5,
                          hp=hp, dh=dh),
        grid=(b, hg, l // qb), in_specs=in_specs,
        out_specs=pl.BlockSpec((1, qb, hw), lambda bi, hi, qi: (bi, qi, hi)),
        out_shape=jax.ShapeDtypeStruct((b, l, n_heads * dh), BF16),
        compiler_params=_cparams(("arbitrary", "arbitrary", "arbitrary")),
        name="sb_attention",
    )(*ins)


def _chunk_rows(u_ref, mb):
    return jnp.concatenate([u_ref[pl.ds(s, mb, stride=SSM_CHUNK), :] for s in range(SSM_CHUNK)], axis=1)


def _s5_inject_kernel(u_ref, bc_ref, re_ref, im_ref, *, mb):
    inj = jnp.dot(_chunk_rows(u_ref, mb), bc_ref[0], precision=HIGHEST, preferred_element_type=F32)
    half = inj.shape[1] // 2
    re_ref[...] = inj[:, :half]
    im_ref[...] = inj[:, half:]


def _s5_scan_kernel(ire_ref, iim_ref, h0r_ref, h0i_ref, ar_ref, ai_ref,
                    sre_ref, sim_ref, fre_ref, fim_ref):
    n_chunks = ire_ref.shape[1]
    ar = ar_ref[...]
    ai = ai_ref[...]

    def body(k, h):
        hr, hi = h
        row = pl.ds(k, 1)
        sre_ref[0, row, :] = hr
        sim_ref[0, row, :] = hi
        return (ar * hr - ai * hi + ire_ref[0, row, :], ar * hi + ai * hr + iim_ref[0, row, :])

    hr, hi = lax.fori_loop(0, n_chunks, body, (h0r_ref[0], h0i_ref[0]))
    fre_ref[0] = hr
    fim_ref[0] = hi


def _s5_output_kernel(u_ref, m_ref, sre_ref, sim_ref, cc_ref, y_ref, *, mb):
    ucat = _chunk_rows(u_ref, mb).astype(BF16)
    half = cc_ref.shape[1] // 2
    y = (jnp.dot(ucat, m_ref[0], preferred_element_type=F32)
         + jnp.dot(sre_ref[...].astype(BF16), cc_ref[0, :half, :], preferred_element_type=F32)
         + jnp.dot(sim_ref[...].astype(BF16), cc_ref[0, half:, :], preferred_element_type=F32))
    for t in range(SSM_CHUNK):
        y_ref[pl.ds(t, mb, stride=SSM_CHUNK), :] = y[:, t * LANES:(t + 1) * LANES]


def s5_matrices(a_re, a_im, log_dt, b_re, b_im, c_re, c_im):
    g, p = a_re.shape
    n_in = b_re.shape[-1]
    ch = SSM_CHUNK
    gl = LANES // n_in
    lg = g // gl
    dt = jnp.exp(log_dt)[:, None]
    mag = jnp.exp(a_re * dt)
    abr, abi = mag * jnp.cos(a_im * dt), mag * jnp.sin(a_im * dt)
    den = a_re * a_re + a_im * a_im
    nr, ni = abr - 1.0, abi
    z_re, z_im = (nr * a_re + ni * a_im) / den, (ni * a_re - nr * a_im) / den
    bb_re = z_re[..., None] * b_re - z_im[..., None] * b_im
    bb_im = z_re[..., None] * b_im + z_im[..., None] * b_re
    steps = jnp.arange(ch + 1, dtype=F32)[:, None, None]
    pmag = jnp.exp(steps * (a_re * dt)[None])
    pw_re = pmag * jnp.cos(steps * (a_im * dt)[None])
    pw_im = pmag * jnp.sin(steps * (a_im * dt)[None])
    e_re = c_re[None] * pw_re[:, :, None, :] - c_im[None] * pw_im[:, :, None, :]
    e_im = c_re[None] * pw_im[:, :, None, :] + c_im[None] * pw_re[:, :, None, :]
    kern = (jnp.einsum('ngop,gpi->ngoi', e_re[:ch], bb_re, precision=HIGHEST)
            - jnp.einsum('ngop,gpi->ngoi', e_im[:ch], bb_im, precision=HIGHEST))
    tile_w = gl * n_in
    grp = lambda n, width: lax.broadcasted_iota(jnp.int32, (n,), 0) // width

    def same_group(row_grp, col_grp):
        return row_grp[:, None] == col_grp[None, :]

    z = kern.reshape(ch, lg, gl, n_in, n_in).transpose(1, 0, 2, 4, 3).reshape(lg, ch, tile_w, n_in)
    kc = jnp.concatenate([z] * gl, axis=-1)
    kc = jnp.where(same_group(grp(tile_w, n_in), grp(tile_w, n_in)), kc, 0.0).astype(BF16)
    kcat = jnp.concatenate([kc[:, n] for n in range(ch)], axis=-1)
    m_big = jnp.concatenate(
        [jnp.pad(kcat[:, :, :(ch - s) * tile_w], ((0, 0), (0, 0), (s * tile_w, 0))) for s in range(ch)], axis=1)
    rp_re = jnp.stack([pw_re[ch - 1 - s] for s in range(ch)])
    rp_im = jnp.stack([pw_im[ch - 1 - s] for s in range(ch)])
    bc_re = rp_re[:, :, :, None] * bb_re[None] - rp_im[:, :, :, None] * bb_im[None]
    bc_im = rp_re[:, :, :, None] * bb_im[None] + rp_im[:, :, :, None] * bb_re[None]
    bc_mask = same_group(grp(ch * tile_w, n_in) % gl, grp(gl * p, p))

    def bc_big(x):
        y = x.reshape(ch, lg, gl, p, n_in).transpose(1, 0, 2, 4, 3).reshape(lg, ch * tile_w, p)
        return jnp.where(bc_mask, jnp.concatenate([y] * gl, axis=-1), 0.0)

    bc = jnp.concatenate([bc_big(bc_re), bc_big(bc_im)], axis=2)
    cc_mask = same_group(grp(gl * p, p), grp(ch * tile_w, n_in) % gl)

    def cc_big(x):
        y = x.reshape(ch, lg, gl, n_in, p).transpose(1, 2, 4, 0, 3).reshape(lg, gl * p, ch, n_in)
        y = jnp.concatenate([jnp.concatenate([y[:, :, t]] * gl, axis=-1) for t in range(ch)], axis=-1)
        return jnp.where(cc_mask, y, 0.0)

    cc = jnp.concatenate([cc_big(e_re[1:]), cc_big(-e_im[1:])], axis=1).astype(BF16)
    a_pow = (pw_re[ch].reshape(1, g * p), pw_im[ch].reshape(1, g * p))
    return m_big, bc, cc, a_pow


def s5_ssm(proj2d, u_col0, b, l, h0_re, h0_im, mats):
    m_big, bc, cc, (apr, api) = mats
    g, p = h0_re.shape[1], h0_re.shape[2]
    ch = SSM_CHUNK
    lg = m_big.shape[0]
    w = lg * LANES
    assert l % ch == 0 and u_col0 % LANES == 0
    k = l // ch
    m = k * b
    mb = _tile(m, 256)
    sw = g * p
    swl = sw // lg
    ucol = u_col0 // LANES
    u_spec = pl.BlockSpec((mb * ch, LANES), lambda i, j: (j, ucol + i))
    st_spec = pl.BlockSpec((mb, swl), lambda i, j: (j, i))
    inj_re, inj_im = pl.pallas_call(
        functools.partial(_s5_inject_kernel, mb=mb), grid=(lg, m // mb),
        in_specs=[u_spec, pl.BlockSpec((1,) + bc.shape[1:], lambda i, j: (i, 0, 0))],
        out_specs=[st_spec] * 2,
        out_shape=[jax.ShapeDtypeStruct((m, sw), F32)] * 2,
        compiler_params=_cparams(("arbitrary", "arbitrary")),
        name="s5_inject",
    )(proj2d, bc)
    tw = _tile(sw, 1024, LANES)
    seq = pl.BlockSpec((1, k, tw), lambda i, j: (i, 0, j))
    st = pl.BlockSpec((1, 1, tw), lambda i, j: (i, 0, j))
    par = pl.BlockSpec((1, tw), lambda i, j: (0, j))
    s_re, s_im, f_re, f_im = pl.pallas_call(
        _s5_scan_kernel, grid=(b, sw // tw),
        in_specs=[seq, seq, st, st, par, par],
        out_specs=[seq, seq, st, st],
        out_shape=[jax.ShapeDtypeStruct((b, k, sw), F32)] * 2 + [jax.ShapeDtypeStruct((b, 1, sw), F32)] * 2,
        compiler_params=_cparams(("arbitrary", "arbitrary")),
        name="s5_scan",
    )(inj_re.reshape(b, k, sw), inj_im.reshape(b, k, sw),
      h0_re.reshape(b, 1, sw), h0_im.reshape(b, 1, sw), apr, api)
    y = pl.pallas_call(
        functools.partial(_s5_output_kernel, mb=mb), grid=(lg, m // mb),
        in_specs=[u_spec,
                  pl.BlockSpec((1,) + m_big.shape[1:], lambda i, j: (i, 0, 0)),
                  st_spec, st_spec,
                  pl.BlockSpec((1,) + cc.shape[1:], lambda i, j: (i, 0, 0))],
        out_specs=pl.BlockSpec((mb * ch, LANES), lambda i, j: (j, i)),
        out_shape=jax.ShapeDtypeStruct((b * l, w), F32),
        compiler_params=_cparams(("arbitrary", "arbitrary")),
        name="s5_output",
    )(proj2d, m_big, s_re.reshape(m, sw), s_im.reshape(m, sw), cc)
    return y, f_re.reshape(b, g, p), f_im.reshape(b, g, p)


def _ssm_glu_kernel(y_ref, u_ref, d_ref, w_ref, o_ref):
    gact = _gelu(y_ref[...] + d_ref[...] * u_ref[...])
    gate = _sigmoid(jnp.dot(gact.astype(BF16), w_ref[0], preferred_element_type=F32))
    o_ref[...] = (gact * gate).astype(o_ref.dtype)


def ssm_glu(y2d, proj2d, u_col_block, d, w_glu_all, layer):
    m, w = y2d.shape
    tm = _tile(m, 512)
    return pl.pallas_call(
        _ssm_glu_kernel, grid=(m // tm,),
        in_specs=[pl.BlockSpec((tm, w), lambda i: (i, 0)),
                  pl.BlockSpec((tm, w), lambda i: (i, u_col_block)),
                  pl.BlockSpec((1, w), lambda i: (0, 0)),
                  pl.BlockSpec((1, w, w), lambda i: (layer, 0, 0))],
        out_specs=pl.BlockSpec((tm, w), lambda i: (i, 0)),
        out_shape=jax.ShapeDtypeStruct((m, w), BF16),
        compiler_params=_cparams(("arbitrary",)),
        name="ssm_glu",
    )(y2d, proj2d, d.reshape(1, w), w_glu_all)


def _lru_kernel(x_ref, gate_ref, hist_ref, h0_ref, cw_ref, cb_ref, wr_ref, br_ref, wi_ref, bi_ref,
                lam_ref, o_ref, hl_ref, nh_ref, xbuf, a_buf, b_buf, hcar, *, tt, n_taps):
    ti = pl.program_id(1)
    pad = SUBLANES
    nh = n_taps - 1

    @pl.when(ti == 0)
    def _():
        xbuf[pad - nh:pad, :] = hist_ref[0]
        hcar[...] = h0_ref[0]

    xbuf[pad:pad + tt, :] = x_ref[0]
    xc = cb_ref[...] + cw_ref[0:1, :] * xbuf[pad - nh:pad - nh + tt, :]
    for k in range(1, n_taps):
        xc = xc + cw_ref[k:k + 1, :] * xbuf[pad - nh + k:pad - nh + k + tt, :]
    last = xbuf[pad + tt - nh:pad + tt, :]
    nh_ref[0] = last
    xbuf[pad - nh:pad, :] = last

    nb = wr_ref.shape[0]
    bw = wr_ref.shape[1]
    xcb = xc.astype(BF16)
    r_parts, i_parts = [], []
    for blk in range(nb):
        xs = xcb[:, blk * bw:(blk + 1) * bw]
        r_parts.append(jnp.dot(xs, wr_ref[blk], preferred_element_type=F32))
        i_parts.append(jnp.dot(xs, wi_ref[blk], preferred_element_type=F32))
    r = _sigmoid(jnp.concatenate(r_parts, axis=1) + br_ref[...])
    ig = _sigmoid(jnp.concatenate(i_parts, axis=1) + bi_ref[...])
    nlam = -lam_ref[...]
    softplus = jnp.maximum(nlam, 0.0) + jnp.log1p(jnp.exp(-jnp.abs(nlam)))
    log_a = -LRU_C * r * softplus
    a_buf[...] = jnp.exp(log_a)
    th = jnp.tanh(log_a)
    b_buf[...] = jnp.sqrt(-2.0 * th / (1.0 - th)) * ig * xc

    def body(t, h):
        h = a_buf[pl.ds(t, 1), :] * h + b_buf[pl.ds(t, 1), :]
        b_buf[pl.ds(t, 1), :] = h
        return h

    h = lax.fori_loop(0, tt, body, hcar[...])
    hcar[...] = h
    hl_ref[0] = h
    o_ref[0] = (b_buf[...] * _gelu(gate_ref[0])).astype(o_ref.dtype)


def lru_branch(proj, hist, h0, cw, cb, wr, br, wi, bi, lam):
    b, l, _ = proj.shape
    w = h0.shape[-1]
    n_taps = cw.shape[0]
    tt = _tile(l, 256)
    row = lambda a: a.reshape(1, w)
    par = pl.BlockSpec((1, w), lambda i, j: (0, 0))
    wspec = pl.BlockSpec(wr.shape, lambda i, j: (0, 0, 0))
    return pl.pallas_call(
        functools.partial(_lru_kernel, tt=tt, n_taps=n_taps),
        grid=(b, l // tt),
        in_specs=[pl.BlockSpec((1, tt, w), lambda i, j: (i, j, 0)),
                  pl.BlockSpec((1, tt, w), lambda i, j: (i, j, 1)),
                  pl.BlockSpec((1, n_taps - 1, w), lambda i, j: (i, 0, 0)),
                  pl.BlockSpec((1, 1, w), lambda i, j: (i, 0, 0)),
                  pl.BlockSpec((n_taps, w), lambda i, j: (0, 0)),
                  par, wspec, par, wspec, par, par],
        out_specs=[pl.BlockSpec((1, tt, w), lambda i, j: (i, j, 0)),
                   pl.BlockSpec((1, 1, w), lambda i, j: (i, 0, 0)),
                   pl.BlockSpec((1, n_taps - 1, w), lambda i, j: (i, 0, 0))],
        out_shape=[jax.ShapeDtypeStruct((b, l, w), BF16),
                   jax.ShapeDtypeStruct((b, 1, w), F32),
                   jax.ShapeDtypeStruct((b, n_taps - 1, w), F32)],
        scratch_shapes=[pltpu.VMEM((SUBLANES + tt, w), F32), pltpu.VMEM((tt, w), F32),
                        pltpu.VMEM((tt, w), F32), pltpu.VMEM((1, w), F32)],
        compiler_params=_cparams(("arbitrary", "arbitrary")),
        name="lru_branch",
    )(proj, proj, hist, h0.reshape(b, 1, w), cw, row(cb), wr, row(br), wi, row(bi), row(lam))


def _conf_kernel(dv_ref, dg_ref, hist_ref, cw_ref, cb_ref, g_ref, b_ref, o_ref, nh_ref, gbuf, dcbuf,
                 *, tt, n_taps, pad):
    ti = pl.program_id(1)
    nh = n_taps - 1

    @pl.when(ti == 0)
    def _():
        gbuf[pad - nh:pad, :] = hist_ref[0]

    gbuf[pad:pad + tt, :] = dv_ref[0] * _sigmoid(dg_ref[0])

    def lane_chunk(c, carry):
        c0 = pl.multiple_of(c * LANES, LANES)
        lanes = pl.ds(c0, LANES)
        acc = cb_ref[:, lanes] + cw_ref[0:1, lanes] * gbuf[pad - nh:pad - nh + tt, lanes]
        for k in range(1, n_taps):
            acc = acc + cw_ref[k:k + 1, lanes] * gbuf[pad - nh + k:pad - nh + k + tt, lanes]
        dcbuf[:, lanes] = acc
        return carry

    lax.fori_loop(0, gbuf.shape[1] // LANES, lane_chunk, 0)
    dc = dcbuf[...]
    mu = jnp.mean(dc, axis=-1, keepdims=True)
    yc = dc - mu
    var = jnp.mean(yc * yc, axis=-1, keepdims=True)
    y = yc * lax.rsqrt(var + LN_EPS) * g_ref[...] + b_ref[...]
    o_ref[0] = (y * _sigmoid(y)).astype(o_ref.dtype)
    last = gbuf[pad + tt - nh:pad + tt, :]
    nh_ref[0] = last
    gbuf[pad - nh:pad, :] = last


def conf_branch(proj, hist, cw, cb, g, bta):
    b, l, _ = proj.shape
    w = hist.shape[-1]
    n_taps = cw.shape[0]
    tt = _tile(l, 256)
    assert tt >= n_taps - 1 and w % LANES == 0
    pad = -(-(n_taps - 1) // SUBLANES) * SUBLANES
    par = pl.BlockSpec((1, w), lambda i, j: (0, 0))
    row = lambda a: a.reshape(1, w)
    return pl.pallas_call(
        functools.partial(_conf_kernel, tt=tt, n_taps=n_taps, pad=pad),
        grid=(b, l // tt),
        in_specs=[pl.BlockSpec((1, tt, w), lambda i, j: (i, j, 2)),
                  pl.BlockSpec((1, tt, w), lambda i, j: (i, j, 3)),
                  pl.BlockSpec((1, n_taps - 1, w), lambda i, j: (i, 0, 0)),
                  pl.BlockSpec((n_taps, w), lambda i, j: (0, 0)),
                  par, par, par],
        out_specs=[pl.BlockSpec((1, tt, w), lambda i, j: (i, j, 0)),
                   pl.BlockSpec((1, n_taps - 1, w), lambda i, j: (i, 0, 0))],
        out_shape=[jax.ShapeDtypeStruct((b, l, w), BF16),
                   jax.ShapeDtypeStruct((b, n_taps - 1, w), F32)],
        scratch_shapes=[pltpu.VMEM((pad + tt, w), F32), pltpu.VMEM((tt, w), F32)],
        compiler_params=_cparams(("arbitrary", "arbitrary")),
        name="conf_branch",
    )(proj, proj, hist, cw, row(cb), row(g), row(bta))


def _extract_topk(vals, payload, k):
    n = vals.shape[0]
    iota = lax.broadcasted_iota(jnp.int32, vals.shape, 0).astype(F32)
    top_v, top_p = [], []
    for _ in range(k):
        m = jnp.max(vals, axis=0, keepdims=True)
        idx = jnp.min(jnp.where(vals == m, iota, float(n)), axis=0, keepdims=True)
        sel = iota == idx
        top_v.append(m)
        if payload is None:
            top_p.append(idx)
        else:
            top_p.append(jnp.max(jnp.where(sel, payload, -1.0), axis=0, keepdims=True))
        vals = jnp.where(sel, -jnp.inf, vals)
    return jnp.concatenate(top_v, axis=0), jnp.concatenate(top_p, axis=0)


def _product_candidates(s1, i1, s2, i2, n_keys):
    k = s1.shape[0]
    sub = SUBLANES
    assert k == 2 * sub
    jrow = lax.broadcasted_iota(jnp.int32, (sub, s1.shape[1]), 0)
    vals = [s1[0:1] + s2]
    ids = [i1[0:1] * n_keys + i2]
    for i in range(1, sub):
        limit = k // (i + 1)
        v = s1[i:i + 1] + s2[0:sub]
        vals.append(v if limit >= sub else jnp.where(jrow < limit, v, -jnp.inf))
        ids.append(i1[i:i + 1] * n_keys + i2[0:sub])
    vals.append(s1[sub:k] + s2[0:1])
    ids.append(i1[sub:k] * n_keys + i2[0:1])
    return jnp.concatenate(vals, axis=0), jnp.concatenate(ids, axis=0)


def _peer_route_kernel(q_ref, keys_ref, e_ref, g_ref, *, n_heads, n_keys, half):
    topk = PEER_TOPK
    for h in range(n_heads):
        tops = []
        for p in range(2):
            c0 = (h * 2 + p) * half
            s = lax.dot_general(keys_ref[p], q_ref[:, c0:c0 + half], (((1,), (1,)), ((), ())),
                                precision=HIGHEST, preferred_element_type=F32)
            tops.append(_extract_topk(s, None, topk))
        (s1, i1), (s2, i2) = tops
        cand, cand_e = _product_candidates(s1, i1, s2, i2, float(n_keys))
        top, e = _extract_topk(cand, cand_e, topk)
        ex = jnp.exp(top - top[0:1])
        gate = ex / jnp.sum(ex, axis=0, keepdims=True)
        e_ref[h * topk:(h + 1) * topk, :] = e.astype(jnp.int32)
        g_ref[h * topk:(h + 1) * topk, :] = gate


def peer_route(q2d, keys):
    t, qw = q2d.shape
    _, n_keys, half = keys.shape
    n_heads = qw // (2 * half)
    tt = _tile(t, 256, LANES)
    rows = n_heads * PEER_TOPK
    return pl.pallas_call(
        functools.partial(_peer_route_kernel, n_heads=n_heads, n_keys=n_keys, half=half),
        grid=(t // tt,),
        in_specs=[pl.BlockSpec((tt, qw), lambda i: (i, 0)),
                  pl.BlockSpec(keys.shape, lambda i: (0, 0, 0))],
        out_specs=[pl.BlockSpec((rows, tt), lambda i: (0, i))] * 2,
        out_shape=[jax.ShapeDtypeStruct((rows, t), jnp.int32), jax.ShapeDtypeStruct((rows, t), F32)],
        compiler_params=_cparams(("arbitrary",)),
        name="peer_route",
    )(q2d, keys)


def _peer_gates_kernel(e_ref, g_ref, w_ref, scr, *, n_keys, tt, group):
    shift = int(math.log2(n_keys))
    sub = lax.broadcasted_iota(jnp.int32, (n_keys, e_ref.shape[1]), 0)

    def body(i, carry):
        base = pl.multiple_of(i * group, group)
        for j in range(group):
            e = e_ref[pl.ds(base + j, 1), :]
            gt = g_ref[pl.ds(base + j, 1), :]
            a_hot = jnp.where((e >> shift) == sub, 1.0, 0.0).astype(BF16)
            b_hot = jnp.where((e & (n_keys - 1)) == sub, gt, 0.0).astype(BF16)
            w = lax.dot_general(a_hot, b_hot, (((1,), (1,)), ((), ())), preferred_element_type=F32)
            scr[pl.ds(j, n_keys, stride=group), :] = w
        for a in range(n_keys):
            w_ref[pl.ds(base, group), a * n_keys:(a + 1) * n_keys] = (
                scr[a * group:(a + 1) * group, :].astype(w_ref.dtype))
        return carry

    lax.fori_loop(0, tt // group, body, 0)


def peer_gates(e_tm, g_tm, n_keys):
    t, ns = e_tm.shape
    assert n_keys & (n_keys - 1) == 0
    group = 2 * SUBLANES
    tt = _tile(t, 128, group)
    return pl.pallas_call(
        functools.partial(_peer_gates_kernel, n_keys=n_keys, tt=tt, group=group),
        grid=(t // tt,),
        in_specs=[pl.BlockSpec((tt, ns), lambda i: (i, 0))] * 2,
        out_specs=pl.BlockSpec((tt, n_keys * n_keys), lambda i: (i, 0)),
        out_shape=jax.ShapeDtypeStruct((t, n_keys * n_keys), BF16),
        scratch_shapes=[pltpu.VMEM((n_keys * group, n_keys), F32)],
        compiler_params=_cparams(("arbitrary",)),
        name="peer_gates",
    )(e_tm, g_tm)


def _peer_dense_kernel(x_ref, u_ref, v_ref, w_ref, o_ref):
    @pl.when(pl.program_id(1) == 0)
    def _():
        o_ref[...] = jnp.zeros_like(o_ref)

    hid = lax.dot_general(x_ref[...], u_ref[0], (((1,), (1,)), ((), ())), preferred_element_type=F32)
    act = (w_ref[...].astype(F32) * _gelu(hid)).astype(BF16)
    o_ref[...] += jnp.dot(act, v_ref[0], preferred_element_type=F32)


def peer_dense(x2d, u_all, v_all, layer, w2d):
    t, d = x2d.shape
    e = v_all.shape[1]
    tt = _tile(t, 512)
    ec = _tile(e, 512, LANES)
    return pl.pallas_call(
        _peer_dense_kernel, grid=(t // tt, e // ec),
        in_specs=[pl.BlockSpec((tt, d), lambda i, c: (i, 0)),
                  pl.BlockSpec((1, ec, d), lambda i, c: (layer, c, 0)),
                  pl.BlockSpec((1, ec, d), lambda i, c: (layer, c, 0)),
                  pl.BlockSpec((tt, ec), lambda i, c: (i, c))],
        out_specs=pl.BlockSpec((tt, d), lambda i, c: (i, 0)),
        out_shape=jax.ShapeDtypeStruct((t, d), F32),
        compiler_params=_cparams(("arbitrary", "arbitrary")),
        name="peer_dense",
    )(x2d, u_all, v_all, w2d)


def peer(h, p, layer):
    b, l, d = h.shape
    t = b * l
    h2d = h.reshape(t, d)
    q = matmul([h2d], p['peer_wq'], layer)
    keys = p['peer_keys'][layer]
    e_t, g_t = peer_route(q, keys)
    w = peer_gates(e_t.T, g_t.T, keys.shape[1])
    return peer_dense(h2d, p['peer_u'], p['peer_v'], layer, w).reshape(b, l, d)


def _even_mixer(h, past, s_re, s_im, p, e):
    b, l, d = h.shape
    n_heads, dh = p['n_heads'], p['dh']
    sbw = n_heads * dh
    proj = matmul([h.reshape(b * l, d)], p['w_in_e'], e)
    ncols = proj.shape[1]
    proj3 = proj.reshape(b, l, ncols)
    att = sb_attention(proj3, n_heads, dh, past)
    ssw = ncols - 3 * sbw
    y, s_re, s_im = s5_ssm(proj, 3 * sbw, b, l, s_re, s_im, p['s5_mats'][e])
    assert (3 * sbw) % ssw == 0
    ssm_out = ssm_glu(y, proj, 3 * sbw // ssw, p['ssm_d'][e], p['ssm_w_glu'], e)
    out = matmul([att.reshape(b * l, sbw), ssm_out], p['w_out_e'], e).reshape(b, l, d)
    k = proj3[..., sbw:2 * sbw].reshape(b, l, n_heads, dh)
    v = proj3[..., 2 * sbw:3 * sbw].reshape(b, l, n_heads, dh)
    return out, k, v, s_re, s_im


def _odd_mixer(h, lru_h, lru_hist, d_hist, p, o):
    b, l, d = h.shape
    proj = matmul([h.reshape(b * l, d)], p['w_in_o'], o)
    proj3 = proj.reshape(b, l, proj.shape[1])
    lru_out, h_last, new_lru_hist = lru_branch(
        proj3, lru_hist, lru_h, p['lru_conv_w'][o], p['lru_conv_b'][o], p['lru_w_r'][o], p['lru_b_r'][o],
        p['lru_w_i'][o], p['lru_b_i'][o], p['lru_lam'][o])
    conf_out, new_d_hist = conf_branch(proj3, d_hist, p['dconv_w'][o], p['dconv_b'][o],
                                       p['dnorm_g'][o], p['dnorm_b'][o])
    w = lru_out.shape[-1]
    out = matmul([lru_out.reshape(b * l, w), conf_out.reshape(b * l, conf_out.shape[-1])],
                 p['w_out_o'], o).reshape(b, l, d)
    return out, h_last.reshape(b, w), new_lru_hist, new_d_hist


def _trunk(x, mod, k_past, v_past, ssm_re, ssm_im, lru_h, lru_hist, d_hist, p):
    depth = p['depth']
    alpha = (2.0 * depth) ** 0.25
    d = x.shape[-1]

    def mod_of(l, s):
        m = mod[l * 2 + s][:, None, :]
        return m[..., :d], m[..., d:2 * d], m[..., 2 * d:]

    ks, vs, sres, sims, lhs, lcs, dcs = [], [], [], [], [], [], []
    shift, scale, gate = mod_of(0, 0)
    h = modulate(x, scale, shift)
    for l in range(depth):
        if l % 2 == 0:
            e = l // 2
            past = None if k_past is None else (k_past, v_past, e)
            out, k, v, s_re, s_im = _even_mixer(h, past, ssm_re[e], ssm_im[e], p, e)
            ks.append(k)
            vs.append(v)
            sres.append(s_re)
            sims.append(s_im)
        else:
            o = l // 2
            out, hl, lc, dc = _odd_mixer(h, lru_h[o], lru_hist[o], d_hist[o], p, o)
            lhs.append(hl)
            lcs.append(lc)
            dcs.append(dc)
        shift2, scale2, gate2 = mod_of(l, 1)
        x, h = res_ln(x, out, gate, p['ln_g'][l, 0], p['ln_b'][l, 0], alpha, nxt=(scale2, shift2))
        out = peer(h, p, l)
        if l + 1 < depth:
            shift, scale, gate_n = mod_of(l + 1, 0)
            x, h = res_ln(x, out, gate2, p['ln_g'][l, 1], p['ln_b'][l, 1], alpha, nxt=(scale, shift))
            gate = gate_n
        else:
            x, _ = res_ln(x, out, gate2, p['ln_g'][l, 1], p['ln_b'][l, 1], alpha)
    return (x, jnp.stack(ks), jnp.stack(vs), jnp.stack(sres), jnp.stack(sims),
            jnp.stack(lhs), jnp.stack(lcs), jnp.stack(dcs))


def kernel(x_prompt, x_sample, c_prompt, c_sample, cache_k, cache_v, state_ssm_re, state_ssm_im, state_lru, state_lru_conv, state_dconv, ada_w, ada_b, ln_g, ln_b, w_in_e, w_out_e, ssm_a_re, ssm_a_im, ssm_log_dt, ssm_b_re, ssm_b_im, ssm_c_re, ssm_c_im, ssm_d, ssm_w_glu, w_in_o, w_out_o, lru_conv_w, lru_conv_b, lru_w_r, lru_b_r, lru_w_i, lru_b_i, lru_lam, dconv_w, dconv_b, dnorm_g, dnorm_b, peer_wq, peer_keys, peer_u, peer_v):
    depth = ada_w.shape[0]
    d = x_prompt.shape[-1]
    n_even, n_odd = w_in_e.shape[0], w_in_o.shape[0]
    bp = x_prompt.shape[0]
    n_heads, dh = cache_k.shape[3], cache_k.shape[4]
    g, pst = state_ssm_re.shape[2], state_ssm_re.shape[3]
    lw = state_lru.shape[-1]

    p = dict(
        depth=depth, n_heads=n_heads, dh=dh, ln_g=ln_g, ln_b=ln_b,
        w_in_e=w_in_e.astype(BF16), w_out_e=w_out_e.astype(BF16),
        w_in_o=w_in_o.astype(BF16), w_out_o=w_out_o.astype(BF16),
        ssm_d=ssm_d, ssm_w_glu=ssm_w_glu.astype(BF16),
        s5_mats=[s5_matrices(ssm_a_re[e], ssm_a_im[e], ssm_log_dt[e], ssm_b_re[e], ssm_b_im[e],
                             ssm_c_re[e], ssm_c_im[e]) for e in range(n_even)],
        lru_conv_w=lru_conv_w, lru_conv_b=lru_conv_b, lru_w_r=lru_w_r.astype(BF16), lru_b_r=lru_b_r,
        lru_w_i=lru_w_i.astype(BF16), lru_b_i=lru_b_i, lru_lam=lru_lam,
        dconv_w=dconv_w, dconv_b=dconv_b, dnorm_g=dnorm_g, dnorm_b=dnorm_b,
        peer_wq=peer_wq.astype(BF16), peer_keys=peer_keys,
        peer_u=peer_u.astype(BF16), peer_v=peer_v.astype(BF16),
    )

    c_all = jnp.concatenate([c_prompt, c_sample], axis=0)
    mod = ada_modulation(c_all, ada_w.reshape(depth * 2, d, 3 * d), ada_b.reshape(depth * 2, 1, 3 * d))
    mod_p = [mod[i, :bp] for i in range(depth * 2)]
    mod_s = [mod[i, bp:] for i in range(depth * 2)]

    ssm0 = jnp.zeros((n_even, bp, g, pst), F32)
    lru0 = jnp.zeros((n_odd, bp, lw), F32)
    lconv0 = jnp.zeros((n_odd, bp) + state_lru_conv.shape[2:], F32)
    dconv0 = jnp.zeros((n_odd, bp) + state_dconv.shape[2:], F32)
    (y_p, k_p, v_p, sre_p, sim_p, lru_p, lc_p, dc_p) = _trunk(
        x_prompt, mod_p, None, None, ssm0, ssm0, lru0, lconv0, dconv0, p)
    (y_s, k_s, v_s, sre_s, sim_s, lru_s, lc_s, dc_s) = _trunk(
        x_sample, mod_s, cache_k, cache_v, state_ssm_re, state_ssm_im, state_lru, state_lru_conv, state_dconv, p)
    return (y_p, y_s, k_p, v_p, k_s, v_s, sre_p, sim_p, sre_s, sim_s,
            lru_p, lru_s, lc_p, lc_s, dc_p, dc_s)
```
